```python
import math
import jax, jax.numpy as jnp
from jax import lax
import numpy as np

D_MODEL = 1024
BATCH = 8
SEQ = 4096
DEPTH = 2

HEAD_DIM = 64
HY_WIDTH = D_MODEL // 2
HY_ORDER = 2
HY_FILT_HIDDEN = 64
HY_POS_BANDS = 16
HY_EMB = 1 + 2 * HY_POS_BANDS
HY_FAST_DECAY_PCT = 0.3
HY_SLOW_DECAY_PCT = 1.5
HY_DECAY_TARGET = 1e-2
WINDOWS = (128, 512, 2048)
DILATIONS = (1, 4, 16)
N_GROUPS = 3
HEADS_PER_GROUP = 4
N_ATT_HEADS = N_GROUPS * HEADS_PER_GROUP
ATT_WIDTH = N_ATT_HEADS * HEAD_DIM
ATT_OUT = HEADS_PER_GROUP * HEAD_DIM
REL_BUCKETS = 32
REL_MAX_DISTANCE = 1024
D_FF = (8 * D_MODEL // 3) // 128 * 128
N_BRANCH = 2
IN_WIDTH = 3 * HY_WIDTH + 3 * ATT_WIDTH
EPS = 1e-6
NEG = -1e30

kernel_name = "hybrid_hyena_dilated_attn_macaron"


def rms_norm(x, g):
    xf = x.astype(jnp.float32)
    y = xf * lax.rsqrt(jnp.mean(xf * xf, axis=-1, keepdims=True) + EPS)
    return (y * g.astype(jnp.float32)).astype(x.dtype)


def swiglu(h, w_gate, w_up, w_down):
    return (jax.nn.silu(h @ w_gate) * (h @ w_up)) @ w_down


def short_conv3(u, w, b):
    up = jnp.pad(u, ((0, 0), (1, 1), (0, 0)))
    return up[:, :-2] * w[0] + up[:, 1:-1] * w[1] + up[:, 2:] * w[2] + b


def hyena_filters(L, w1, b1, w2, b2, w3):
    f32 = jnp.float32
    t = jnp.linspace(0.0, 1.0, L, dtype=f32)[:, None]
    w = (2.0 * math.pi / L) * jnp.arange(L, dtype=f32)[:, None]
    f = jnp.linspace(1e-4, HY_POS_BANDS - 1, HY_POS_BANDS, dtype=f32)[None]
    z = jnp.concatenate([t, jnp.cos(f * w), -jnp.sin(f * w)], axis=-1)
    hid = jnp.sin(z @ w1.astype(f32) + b1.astype(f32))
    hid = jnp.sin(hid @ w2.astype(f32) + b2.astype(f32))
    k = (hid @ w3.astype(f32)).reshape(L, HY_ORDER, 2, HY_WIDTH)
    max_decay = math.log(HY_DECAY_TARGET) / HY_FAST_DECAY_PCT
    min_decay = math.log(HY_DECAY_TARGET) / HY_SLOW_DECAY_PCT
    deltas = jnp.abs(jnp.linspace(min_decay, max_decay, HY_WIDTH, dtype=f32))
    k = k * jnp.exp(-t * deltas[None])[:, None, None, :]
    full = jnp.concatenate([k[:, :, 0], jnp.zeros((1, HY_ORDER, HY_WIDTH), f32), k[:0:-1, :, 1]], axis=0)
    full = full * lax.rsqrt(jnp.sum(full * full, axis=0, keepdims=True) + EPS)
    return jnp.fft.rfft(full, axis=0)


def long_conv(u, kf):
    L = u.shape[1]
    U = jnp.fft.rfft(u.astype(jnp.float32), n=2 * L, axis=1)
    return jnp.fft.irfft(U * kf[None], n=2 * L, axis=1)[:, :L]


def t5_bucket(rel):
    half = REL_BUCKETS // 2
    exact = half // 2
    ret = jnp.where(rel > 0, half, 0)
    n = jnp.abs(rel)
    nf = jnp.maximum(n, 1).astype(jnp.float32)
    large = exact + (jnp.log(nf / exact) / math.log(REL_MAX_DISTANCE / exact) * (half - exact)).astype(jnp.int32)
    large = jnp.minimum(large, half - 1)
    return ret + jnp.where(n < exact, n, large)


def dilated_group_attn(q, k, v, bias_table, dilation, n_side):
    b, s, h, dh = q.shape
    m = s // dilation
    nb = -(-m // n_side)
    mp = nb * n_side
    bd = b * dilation

    def by_residue(t):
        return t.reshape(b, m, dilation, h, dh).transpose(0, 2, 1, 3, 4).reshape(bd, m, h, dh)

    def key_windows(t):
        tp = jnp.pad(t, ((0, 0), (n_side, n_side + mp - m), (0, 0), (0, 0))).reshape(bd, nb + 2, n_side, h, dh)
        return jnp.concatenate([tp[:, :-2], tp[:, 1:-1], tp[:, 2:]], axis=2)

    qb = jnp.pad(by_residue(q), ((0, 0), (0, mp - m), (0, 0), (0, 0))).reshape(bd, nb, n_side, h, dh)
    kw = key_windows(by_residue(k))
    vw = key_windows(by_residue(v))
    scores = jnp.einsum('bnqhd,bnkhd->bhnqk', qb.astype(jnp.float32), kw.astype(jnp.float32)) * (HEAD_DIM ** -0.5)
    qi = jnp.arange(n_side)
    ki = jnp.arange(3 * n_side)
    rel = ki[None, :] - n_side - qi[:, None]
    key_pos = jnp.arange(nb)[:, None] * n_side + ki[None, :] - n_side
    valid = (jnp.abs(rel) <= n_side)[None] & ((key_pos >= 0) & (key_pos < m))[:, None, :]
    bias = bias_table[t5_bucket(rel * dilation)].astype(jnp.float32).transpose(2, 0, 1)
    scores = jnp.where(valid, scores + bias[:, None], NEG)
    mx = jnp.max(scores, axis=-1, keepdims=True)
    p = jnp.exp(scores - mx)
    denom = jnp.sum(p, axis=-1, keepdims=True)
    out = jnp.einsum('bhnqk,bnkhd->bnqhd', p, vw.astype(jnp.float32)) / denom.transpose(0, 2, 3, 1, 4)
    lse = (mx + jnp.log(denom))[..., 0].transpose(0, 2, 3, 1)
    out = out.reshape(bd, mp, h, dh)[:, :m].reshape(b, dilation, m, h, dh).transpose(0, 2, 1, 3, 4).reshape(b, s, h, dh)
    lse = lse.reshape(bd, mp, h)[:, :m].reshape(b, dilation, m, h).transpose(0, 2, 1, 3).reshape(b, s, h)
    return out, lse


def hybrid_mixer(h, rel_bias, w_in, w_gate, b_gate, hy_conv_w, hy_conv_b, hy_filt_w1, hy_filt_b1,
                 hy_filt_w2, hy_filt_b2, hy_filt_w3, hy_skip, q_norm, k_norm, w_hy_proj, w_at_proj, w_out):
    b, s, _ = h.shape
    proj = h @ w_in
    hy = short_conv3(proj[..., :3 * HY_WIDTH], hy_conv_w, hy_conv_b)
    z, x1, x2 = jnp.split(hy, 3, axis=-1)
    kf = hyena_filters(s, hy_filt_w1, hy_filt_b1, hy_filt_w2, hy_filt_b2, hy_filt_w3)
    for o, gate in enumerate((x1, x2)):
        z = gate * (long_conv(z, kf[:, o]).astype(z.dtype) + hy_skip[o] * z)
    y_hy = z
    qkv = proj[..., 3 * HY_WIDTH:].reshape(b, s, 3, N_GROUPS, HEADS_PER_GROUP, HEAD_DIM)
    q = rms_norm(qkv[:, :, 0], q_norm)
    k = rms_norm(qkv[:, :, 1], k_norm)
    v = qkv[:, :, 2]
    outs, lses = [], []
    for g in range(N_GROUPS):
        n_side = WINDOWS[g] // (2 * DILATIONS[g])
        o_g, l_g = dilated_group_attn(q[:, :, g], k[:, :, g], v[:, :, g],
                                      rel_bias[:, g * HEADS_PER_GROUP:(g + 1) * HEADS_PER_GROUP],
                                      DILATIONS[g], n_side)
        outs.append(o_g)
        lses.append(l_g)
    alpha = jax.nn.softmax(jnp.stack(lses), axis=0)
    y_at = jnp.einsum('gbsh,gbshd->bshd', alpha, jnp.stack(outs)).reshape(b, s, ATT_OUT).astype(h.dtype)
    gates = jax.nn.sigmoid(h @ w_gate + b_gate).reshape(b, s, N_BRANCH, D_MODEL)
    y = gates[:, :, 0] * (y_hy @ w_hy_proj) + gates[:, :, 1] * (y_at @ w_at_proj)
    return y @ w_out


def setup_inputs(seed: int = 0) -> dict:
    key = jax.random.key(seed)
    ks = jax.random.split(key, 32)
    f32 = jnp.float32

    def nrm(k, shape, scale):
        return jax.random.normal(k, shape, f32) * scale

    def gain(k, shape):
        return 1.0 + 0.01 * jax.random.normal(k, shape, f32)

    L = DEPTH
    return {
        'x': nrm(ks[0], (BATCH, SEQ, D_MODEL), 1.0),
        'rel_bias': nrm(ks[1], (REL_BUCKETS, N_ATT_HEADS), 0.5),
        'ffn1_norm': gain(ks[2], (L, D_MODEL)),
        'ffn1_w_gate': nrm(ks[3], (L, D_MODEL, D_FF), D_MODEL ** -0.5),
        'ffn1_w_up': nrm(ks[4], (L, D_MODEL, D_FF), D_MODEL ** -0.5),
        'ffn1_w_down': nrm(ks[5], (L, D_FF, D_MODEL), D_FF ** -0.5),
        'mix_norm': gain(ks[6], (L, D_MODEL)),
        'w_in': nrm(ks[7], (L, D_MODEL, IN_WIDTH), D_MODEL ** -0.5),
        'w_gate': nrm(ks[8], (L, D_MODEL, N_BRANCH * D_MODEL), D_MODEL ** -0.5),
        'b_gate': nrm(ks[9], (L, N_BRANCH * D_MODEL), 0.01),
        'hy_conv_w': nrm(ks[10], (L, 3, 3 * HY_WIDTH), 3 ** -0.5),
        'hy_conv_b': nrm(ks[11], (L, 3 * HY_WIDTH), 0.01),
        'hy_filt_w1': nrm(ks[12], (L, HY_EMB, HY_FILT_HIDDEN), HY_EMB ** -0.5),
        'hy_filt_b1': nrm(ks[13], (L, HY_FILT_HIDDEN), 0.01),
        'hy_filt_w2': nrm(ks[14], (L, HY_FILT_HIDDEN, HY_FILT_HIDDEN), HY_FILT_HIDDEN ** -0.5),
        'hy_filt_b2': nrm(ks[15], (L, HY_FILT_HIDDEN), 0.01),
        'hy_filt_w3': nrm(ks[16], (L, HY_FILT_HIDDEN, HY_ORDER * 2 * HY_WIDTH), HY_FILT_HIDDEN ** -0.5),
        'hy_skip': nrm(ks[17], (L, HY_ORDER, HY_WIDTH), 1.0),
        'q_norm': gain(ks[18], (L, HEAD_DIM)),
        'k_norm': gain(ks[19], (L, HEAD_DIM)),
        'w_hy_proj': nrm(ks[20], (L, HY_WIDTH, D_MODEL), HY_WIDTH ** -0.5),
        'w_at_proj': nrm(ks[21], (L, ATT_OUT, D_MODEL), ATT_OUT ** -0.5),
        'w_out': nrm(ks[22], (L, D_MODEL, D_MODEL), D_MODEL ** -0.5),
        'ffn2_norm': gain(ks[23], (L, D_MODEL)),
        'ffn2_w_gate': nrm(ks[24], (L, D_MODEL, D_FF), D_MODEL ** -0.5),
        'ffn2_w_up': nrm(ks[25], (L, D_MODEL, D_FF), D_MODEL ** -0.5),
        'ffn2_w_down': nrm(ks[26], (L, D_FF, D_MODEL), D_FF ** -0.5),
    }


def reference(x, rel_bias, ffn1_norm, ffn1_w_gate, ffn1_w_up, ffn1_w_down, mix_norm, w_in, w_gate, b_gate,
              hy_conv_w, hy_conv_b, hy_filt_w1, hy_filt_b1, hy_filt_w2, hy_filt_b2, hy_filt_w3, hy_skip,
              q_norm, k_norm, w_hy_proj, w_at_proj, w_out, ffn2_norm, ffn2_w_gate, ffn2_w_up, ffn2_w_down):
    for l in range(DEPTH):
        x = x + 0.5 * swiglu(rms_norm(x, ffn1_norm[l]), ffn1_w_gate[l], ffn1_w_up[l], ffn1_w_down[l])
        h = rms_norm(x, mix_norm[l])
        x = x + hybrid_mixer(h, rel_bias, w_in[l], w_gate[l], b_gate[l], hy_conv_w[l], hy_conv_b[l],
                             hy_filt_w1[l], hy_filt_b1[l], hy_filt_w2[l], hy_filt_b2[l], hy_filt_w3[l],
                             hy_skip[l], q_norm[l], k_norm[l], w_hy_proj[l], w_at_proj[l], w_out[l])
        x = x + 0.5 * swiglu(rms_norm(x, ffn2_norm[l]), ffn2_w_gate[l], ffn2_w_up[l], ffn2_w_down[l])
    return x
```

```python
import functools
import math

import jax
import jax.numpy as jnp
from jax import lax
from jax.experimental import pallas as pl
from jax.experimental.pallas import tpu as pltpu

F32 = jnp.float32
BF16 = jnp.bfloat16

D_MODEL = 1024
SEQ = 4096
HEAD_DIM = 64
HY_WIDTH = 512
HY_ORDER = 2
HY_FILT_HIDDEN = 64
HY_POS_BANDS = 16
HY_EMB = 1 + 2 * HY_POS_BANDS
HY_FAST_DECAY_PCT = 0.3
HY_SLOW_DECAY_PCT = 1.5
HY_DECAY_TARGET = 1e-2
WINDOWS = (128, 512, 2048)
DILATIONS = (1, 4, 16)
N_GROUPS = 3
HEADS_PER_GROUP = 4
ATT_WIDTH = N_GROUPS * HEADS_PER_GROUP * HEAD_DIM
ATT_OUT = HEADS_PER_GROUP * HEAD_DIM
REL_BUCKETS = 32
REL_MAX_DISTANCE = 1024
D_FF = (8 * D_MODEL // 3) // 128 * 128
IN_WIDTH = 3 * HY_WIDTH + 3 * ATT_WIDTH
EPS = 1e-6
NEG = -1e30

SUBLANES = 8
LANES = 128
VMEM_LIMIT_BYTES = 60 * 1024 * 1024

DFT_N = 2 * SEQ
DFT_N2 = 128
DFT_N1 = SEQ // DFT_N2
DFT_F1 = 64
DFT_F2 = SEQ // DFT_F1
DFT_GROUPS = DFT_N2 // SUBLANES
KROWS = DFT_N1 * SUBLANES
UROWS = 2 * DFT_F1 * SUBLANES

N_SIDE = 64
ATT_BQ = 128
ATT_BK = ATT_BQ + 2 * N_SIDE
ATT_PAD = N_SIDE * max(DILATIONS)

FFN_CHUNK = 896
TOKEN_TILE = 256
HY_CB = 128


def _const_spec(shape):
  nd = len(shape)
  return pl.BlockSpec(shape, lambda *_: (0,) * nd, pipeline_mode=pl.Buffered(1))


def _rms(x, g):
  return x * lax.rsqrt(jnp.mean(x * x, axis=-1, keepdims=True) + EPS) * g


def _swiglu_residual(x, g_ref, wg_ref, wu_ref, wd_ref):
  h = _rms(x, g_ref[...]).astype(BF16)
  acc = jnp.zeros_like(x)
  for c0 in range(0, D_FF, FFN_CHUNK):
    a = jnp.dot(h, wg_ref[:, c0:c0 + FFN_CHUNK], preferred_element_type=F32)
    u = jnp.dot(h, wu_ref[:, c0:c0 + FFN_CHUNK], preferred_element_type=F32)
    s = (a * jax.nn.sigmoid(a) * u).astype(BF16)
    acc = acc + jnp.dot(s, wd_ref[c0:c0 + FFN_CHUNK, :], preferred_element_type=F32)
  return x + 0.5 * acc


def _ffn_proj_kernel(x_ref, g1_ref, wg_ref, wu_ref, wd_ref, gm_ref, win_ref, x1_ref, p_ref):
  x1 = _swiglu_residual(x_ref[...], g1_ref, wg_ref, wu_ref, wd_ref)
  x1_ref[...] = x1
  h = _rms(x1, gm_ref[...]).astype(BF16)
  step = IN_WIDTH // 5
  for c0 in range(0, IN_WIDTH, step):
    p_ref[:, c0:c0 + step] = jnp.dot(
        h, win_ref[:, c0:c0 + step], preferred_element_type=F32).astype(BF16)


def _ffn_proj(x, g1, wg, wu, wd, gm, win):
  t = x.shape[0]
  tm = TOKEN_TILE
  tok = lambda w: pl.BlockSpec((tm, w), lambda i: (i, 0))
  return pl.pallas_call(
      _ffn_proj_kernel,
      grid=(t // tm,),
      in_specs=[tok(D_MODEL), _const_spec((1, D_MODEL)), _const_spec((D_MODEL, D_FF)),
                _const_spec((D_MODEL, D_FF)), _const_spec((D_FF, D_MODEL)),
                _const_spec((1, D_MODEL)), _const_spec((D_MODEL, IN_WIDTH))],
      out_specs=[tok(D_MODEL), tok(IN_WIDTH)],
      out_shape=[jax.ShapeDtypeStruct((t, D_MODEL), F32), jax.ShapeDtypeStruct((t, IN_WIDTH), BF16)],
      compiler_params=pltpu.CompilerParams(
          dimension_semantics=("arbitrary",), vmem_limit_bytes=VMEM_LIMIT_BYTES),
      name="ffn_proj",
  )(x, g1, wg, wu, wd, gm, win)


def _merge_ffn_kernel(x_ref, yhy_ref, yat_ref, gm_ref, wgate_ref, bgate_ref, why_ref, wat_ref,
                      wout_ref, g2_ref, wg_ref, wu_ref, wd_ref, o_ref):
  x = x_ref[...]
  h = _rms(x, gm_ref[...]).astype(BF16)
  gates = jax.nn.sigmoid(
      jnp.dot(h, wgate_ref[...], preferred_element_type=F32) + bgate_ref[...])
  a = jnp.dot(yhy_ref[...], why_ref[...], preferred_element_type=F32)
  b = jnp.dot(yat_ref[...], wat_ref[...], preferred_element_type=F32)
  y = gates[:, :D_MODEL] * a + gates[:, D_MODEL:] * b
  x2 = x + jnp.dot(y.astype(BF16), wout_ref[...], preferred_element_type=F32)
  o_ref[...] = _swiglu_residual(x2, g2_ref, wg_ref, wu_ref, wd_ref)


def _merge_ffn(x, yhy, yat, gm, wgate, bgate, why, wat, wout, g2, wg, wu, wd):
  t = x.shape[0]
  tm = TOKEN_TILE
  tok = lambda w: pl.BlockSpec((tm, w), lambda i: (i, 0))
  return pl.pallas_call(
      _merge_ffn_kernel,
      grid=(t // tm,),
      in_specs=[tok(D_MODEL), tok(HY_WIDTH), tok(ATT_OUT), _const_spec((1, D_MODEL)),
                _const_spec((D_MODEL, 2 * D_MODEL)), _const_spec((1, 2 * D_MODEL)),
                _const_spec((HY_WIDTH, D_MODEL)), _const_spec((ATT_OUT, D_MODEL)),
                _const_spec((D_MODEL, D_MODEL)), _const_spec((1, D_MODEL)),
                _const_spec((D_MODEL, D_FF)), _const_spec((D_MODEL, D_FF)),
                _const_spec((D_FF, D_MODEL))],
      out_specs=tok(D_MODEL),
      out_shape=jax.ShapeDtypeStruct((t, D_MODEL), F32),
      compiler_params=pltpu.CompilerParams(
          dimension_semantics=("arbitrary",), vmem_limit_bytes=VMEM_LIMIT_BYTES),
      name="merge_ffn",
  )(x, yhy, yat, gm, wgate, bgate, why, wat, wout, g2, wg, wu, wd)


def _dft_constants():
  two_pi = 2.0 * math.pi
  j = jnp.arange(DFT_GROUPS, dtype=jnp.int32)
  n1 = jnp.arange(DFT_N1, dtype=jnp.int32)
  s = jnp.arange(SUBLANES, dtype=jnp.int32)
  f1 = jnp.arange(DFT_F1, dtype=jnp.int32)
  n = DFT_N2 * n1[None, :, None] + SUBLANES * j[:, None, None] + s[None, None, :]
  ph = ((2 * f1 + 1)[None, :, None, None] * n[:, None, :, :]) % (2 * DFT_N)
  ang = ph.astype(F32) * (two_pi / (2 * DFT_N))
  c, sn = jnp.cos(ang), jnp.sin(ang)
  eye = jnp.eye(SUBLANES, dtype=F32)
  kr = jnp.einsum("jfns,st->jfsnt", c, eye)
  ki = jnp.einsum("jfns,st->jfsnt", -sn, eye)
  mk = jnp.stack([kr, ki], axis=1).reshape(DFT_GROUPS, UROWS, KROWS)
  scale = 2.0 / DFT_N
  gr = jnp.einsum("jfns,st->jntfs", c * scale, eye)
  gi = jnp.einsum("jfns,st->jntfs", -sn * scale, eye)
  mki = jnp.stack([gr, gi], axis=3).reshape(DFT_GROUPS, KROWS, UROWS)
  f2 = jnp.arange(DFT_F2, dtype=jnp.int32)
  n2 = jnp.arange(DFT_N2, dtype=jnp.int32)
  a2 = ((f2[:, None] * n2[None, :]) % DFT_N2).astype(F32) * (two_pi / DFT_N2)
  c2, s2 = jnp.cos(a2), jnp.sin(a2)
  md = jnp.concatenate([jnp.concatenate([c2, s2], axis=1),
                        jnp.concatenate([-s2, c2], axis=1)], axis=0)
  mdi = jnp.concatenate([jnp.concatenate([c2.T, -s2.T], axis=1),
                         jnp.concatenate([s2.T, c2.T], axis=1)], axis=0)
  return mk.astype(BF16), mki.astype(BF16), md.astype(BF16), mdi.astype(BF16)


def _stage_k_forward(z_ref, mk_ref, u_ref):
  def body(j, carry):
    tiles = [z_ref[pl.ds(pl.multiple_of(DFT_N2 * n1 + SUBLANES * j, SUBLANES), SUBLANES), :]
             for n1 in range(DFT_N1)]
    rhs = jnp.concatenate(tiles, axis=0).astype(BF16)
    u_ref[j] = jnp.dot(mk_ref[j], rhs, preferred_element_type=F32)
    return carry

  lax.fori_loop(0, DFT_GROUPS, body, 0)


def _gather_f1(u_ref, f1):
  tiles = []
  for ri in range(2):
    for j in range(DFT_GROUPS):
      row = pl.multiple_of(ri * DFT_F1 * SUBLANES + f1 * SUBLANES, SUBLANES)
      tiles.append(u_ref[j, pl.ds(row, SUBLANES), :])
  return jnp.concatenate(tiles, axis=0)


def _stage_k_inverse(u_ref, mki_ref, y_ref):
  def body(j, carry):
    y = jnp.dot(mki_ref[j], u_ref[j].astype(BF16), preferred_element_type=F32)
    for n1 in range(DFT_N1):
      y_ref[pl.ds(pl.multiple_of(DFT_N2 * n1 + SUBLANES * j, SUBLANES), SUBLANES), :] = (
          y[SUBLANES * n1:SUBLANES * (n1 + 1), :])
    return carry

  lax.fori_loop(0, DFT_GROUPS, body, 0)


def _long_conv(z_ref, y_ref, u_ref, h_ref, order, mk_ref, mki_ref, md_ref, mdi_ref):
  _stage_k_forward(z_ref, mk_ref, u_ref)

  def body(f1, carry):
    x = jnp.dot(md_ref[...], _gather_f1(u_ref, f1).astype(BF16), preferred_element_type=F32)
    xr, xi = x[:DFT_F2], x[DFT_F2:]
    hr = h_ref[order, f1, pl.ds(0, DFT_F2), :]
    hi = h_ref[order, f1, pl.ds(DFT_F2, DFT_F2), :]
    y = jnp.concatenate([xr * hr - xi * hi, xr * hi + xi * hr], axis=0).astype(BF16)
    w = jnp.dot(mdi_ref[...], y, preferred_element_type=F32)
    k = 0
    for ri in range(2):
      for j in range(DFT_GROUPS):
        row = pl.multiple_of(ri * DFT_F1 * SUBLANES + f1 * SUBLANES, SUBLANES)
        u_ref[j, pl.ds(row, SUBLANES), :] = w[SUBLANES * k:SUBLANES * (k + 1), :]
        k += 1
    return carry

  lax.fori_loop(0, DFT_F1, body, 0)
  _stage_k_inverse(u_ref, mki_ref, y_ref)


def _hy_filter_kernel(z_ref, w1_ref, b1_ref, w2_ref, b2_ref, w3f_ref, w3b_ref, dl_ref,
                      mk_ref, md_ref, h_ref, k_ref, u_ref):
  hp = lax.Precision.HIGHEST
  z = z_ref[...]
  hid = jnp.sin(jnp.dot(z, w1_ref[...], precision=hp, preferred_element_type=F32) + b1_ref[...])
  hid = jnp.sin(jnp.dot(hid, w2_ref[...], precision=hp, preferred_element_type=F32) + b2_ref[...])
  decay = jnp.exp(-z[:, 0:1] * dl_ref[...])
  kf = jnp.dot(hid, w3f_ref[...], precision=hp, preferred_element_type=F32) * decay
  kb = jnp.dot(hid, w3b_ref[...], precision=hp, preferred_element_type=F32) * decay
  row = lax.broadcasted_iota(jnp.int32, kb.shape, 0)
  kb = jnp.where(row == 0, 0.0, kb)
  rs = lax.rsqrt(jnp.sum(kf * kf + kb * kb, axis=0, keepdims=True) + EPS)
  c = kf.shape[1]
  k_ref[:, :c] = kf
  k_ref[:, c:] = kb
  _stage_k_forward(k_ref, mk_ref, u_ref)

  def body(f1, carry):
    x = jnp.dot(md_ref[...], _gather_f1(u_ref, f1).astype(BF16), preferred_element_type=F32)
    h_ref[0, f1, pl.ds(0, DFT_F2), :] = (x[:DFT_F2, :c] + x[:DFT_F2, c:]) * rs
    h_ref[0, f1, pl.ds(DFT_F2, DFT_F2), :] = (x[DFT_F2:, :c] - x[DFT_F2:, c:]) * rs
    return carry

  lax.fori_loop(0, DFT_F1, body, 0)


def _hy_filter(zfeat, w1, b1, w2, b2, w3, deltas, mk, md):
  c = HY_CB
  ncb = HY_WIDTH // c
  blocks_per_dir = HY_WIDTH // c
  return pl.pallas_call(
      _hy_filter_kernel,
      grid=(HY_ORDER, ncb),
      in_specs=[_const_spec(zfeat.shape), _const_spec(w1.shape), _const_spec(b1.shape),
                _const_spec(w2.shape), _const_spec(b2.shape),
                pl.BlockSpec((HY_FILT_HIDDEN, c), lambda o, cb: (0, o * 2 * blocks_per_dir + cb)),
                pl.BlockSpec((HY_FILT_HIDDEN, c),
                             lambda o, cb: (0, o * 2 * blocks_per_dir + blocks_per_dir + cb)),
                pl.BlockSpec((1, c), lambda o, cb: (0, cb)),
                _const_spec(mk.shape), _const_spec(md.shape)],
      out_specs=pl.BlockSpec((1, DFT_F1, 2 * DFT_F2, c), lambda o, cb: (o, 0, 0, cb)),
      out_shape=jax.ShapeDtypeStruct((HY_ORDER, DFT_F1, 2 * DFT_F2, HY_WIDTH), F32),
      scratch_shapes=[pltpu.VMEM((SEQ, 2 * c), F32),
                      pltpu.VMEM((DFT_GROUPS, UROWS, 2 * c), F32)],
      compiler_params=pltpu.CompilerParams(
          dimension_semantics=("arbitrary", "arbitrary"), vmem_limit_bytes=VMEM_LIMIT_BYTES),
      name="hy_filter",
  )(zfeat, w1, b1, w2, b2, w3, w3, deltas, mk, md)


def _conv3(p_ref, cw_ref, k):
  u = p_ref[...].astype(F32)
  rows = u.shape[0]
  row = lax.broadcasted_iota(jnp.int32, u.shape, 0)
  prev = jnp.where(row == 0, 0.0, pltpu.roll(u, 1, 0))
  nxt = jnp.where(row == rows - 1, 0.0, pltpu.roll(u, rows - 1, 0))
  w = cw_ref[k]
  return prev * w[0:1] + u * w[1:2] + nxt * w[2:3] + w[3:4]


def _hyena_kernel(pv_ref, px1_ref, px2_ref, cw_ref, skip_ref, h_ref, mk_ref, mki_ref, md_ref,
                  mdi_ref, o_ref, z_ref, y_ref, u_ref):
  z_ref[...] = _conv3(pv_ref, cw_ref, 0)
  gates = (px1_ref, px2_ref)
  for order in range(HY_ORDER):
    _long_conv(z_ref, y_ref, u_ref, h_ref, order, mk_ref, mki_ref, md_ref, mdi_ref)
    gate = _conv3(gates[order], cw_ref, 1 + order)
    z = gate * (y_ref[...] + skip_ref[order:order + 1, :] * z_ref[...])
    if order + 1 < HY_ORDER:
      z_ref[...] = z
    else:
      o_ref[...] = z.astype(BF16)


def _hyena(p, cw, skip, hspec, mk, mki, md, mdi, batch):
  c = HY_CB
  ncb = HY_WIDTH // c
  col = lambda k: pl.BlockSpec((SEQ, c), lambda cb, b: (b, k * ncb + cb))
  return pl.pallas_call(
      _hyena_kernel,
      grid=(ncb, batch),
      in_specs=[col(0), col(1), col(2),
                pl.BlockSpec((3, 4, c), lambda cb, b: (0, 0, cb)),
                pl.BlockSpec((HY_ORDER, c), lambda cb, b: (0, cb)),
                pl.BlockSpec((HY_ORDER, DFT_F1, 2 * DFT_F2, c), lambda cb, b: (0, 0, 0, cb),
                             pipeline_mode=pl.Buffered(1)),
                _const_spec(mk.shape), _const_spec(mki.shape), _const_spec(md.shape),
                _const_spec(mdi.shape)],
      out_specs=pl.BlockSpec((SEQ, c), lambda cb, b: (b, cb)),
      out_shape=jax.ShapeDtypeStruct((batch * SEQ, HY_WIDTH), BF16),
      scratch_shapes=[pltpu.VMEM((SEQ, c), F32), pltpu.VMEM((SEQ, c), F32),
                      pltpu.VMEM((DFT_GROUPS, UROWS, c), F32)],
      compiler_params=pltpu.CompilerParams(
          dimension_semantics=("arbitrary", "arbitrary"), vmem_limit_bytes=VMEM_LIMIT_BYTES),
      name="hyena",
  )(p, p, p, cw, skip, hspec, mk, mki, md, mdi)


def _t5_bucket(rel):
  half = REL_BUCKETS // 2
  exact = half // 2
  ret = jnp.where(rel > 0, half, 0)
  n = jnp.abs(rel)
  nf = jnp.maximum(n, 1).astype(F32)
  large = exact + (jnp.log(nf / exact) / math.log(REL_MAX_DISTANCE / exact) * (half - exact)).astype(jnp.int32)
  large = jnp.minimum(large, half - 1)
  return ret + jnp.where(n < exact, n, large)


def _bucket_maps():
  qi = jnp.arange(ATT_BQ)[:, None]
  ki = jnp.arange(ATT_BK)[None, :]
  rel = ki - N_SIDE - qi
  maps = [jnp.where(jnp.abs(rel) <= N_SIDE, _t5_bucket(rel * d), -1) for d in DILATIONS]
  return jnp.stack(maps).astype(jnp.int32)


def _head_norm(x, gain, lane_lo):
  sq = x * x
  s_lo = jnp.sum(jnp.where(lane_lo, sq, 0.0), axis=-1, keepdims=True)
  s_all = jnp.sum(sq, axis=-1, keepdims=True)
  ms = jnp.where(lane_lo, s_lo, s_all - s_lo) * (1.0 / HEAD_DIM)
  return x * lax.rsqrt(ms + EPS) * gain


def _dil_attn_kernel(tab_ref, *refs):
  qkv_refs = refs[:9]
  bkt_ref, qg_ref, kg_ref, o_ref = refs[9:13]
  qs_ref, ks_ref, vs_ref, acc_ref, m_ref, l_ref, bias_ref = refs[13:]
  pair = pl.program_id(1)
  lane_lo = lax.broadcasted_iota(jnp.int32, (1, LANES), 1) < HEAD_DIM
  key_idx = lax.broadcasted_iota(jnp.int32, (1, ATT_BK), 1)

  for g, d in enumerate(DILATIONS):
    q_ref, k_ref, v_ref = qkv_refs[3 * g:3 * g + 3]
    m_sub = SEQ // d
    pad = N_SIDE * d
    bkt = bkt_ref[g]
    for h in range(2):
      col = g * HEADS_PER_GROUP + 2 * pair + h
      b = jnp.where(bkt < 0, NEG, 0.0)
      for t in range(REL_BUCKETS):
        b = b + jnp.where(bkt == t, tab_ref[t, col], 0.0)
      bias_ref[h] = b
    qs_ref[...] = _head_norm(q_ref[...].astype(F32), qg_ref[...], lane_lo) * (HEAD_DIM ** -0.5)
    zeros = jnp.zeros((pad, LANES), F32)
    ks_ref[pl.ds(0, pad), :] = zeros
    ks_ref[pl.ds(pad + SEQ, pad), :] = zeros
    vs_ref[pl.ds(0, pad), :] = zeros
    vs_ref[pl.ds(pad + SEQ, pad), :] = zeros
    ks_ref[pl.ds(pad, SEQ), :] = _head_norm(k_ref[...].astype(F32), kg_ref[...], lane_lo)
    vs_ref[pl.ds(pad, SEQ), :] = v_ref[...].astype(F32)

    def block(t, carry, d=d, m_sub=m_sub, g=g):
      r = t % d
      i = t // d
      start = r + d * ATT_BQ * i
      if d == 1:
        rows_q = pl.ds(start, ATT_BQ)
        rows_k = pl.ds(start, ATT_BK)
      else:
        rows_q = pl.ds(start, ATT_BQ, stride=d)
        rows_k = pl.ds(start, ATT_BK, stride=d)
      q = qs_ref[rows_q, :]
      k = ks_ref[rows_k, :].astype(BF16)
      v = vs_ref[rows_k, :].astype(BF16)
      kpos = ATT_BQ * i - N_SIDE + key_idx
      kvalid = (kpos >= 0) & (kpos < m_sub)
      outs, maxs, sums = [], [], []
      for h in range(2):
        qh = jnp.where(lane_lo if h == 0 else jnp.logical_not(lane_lo), q, 0.0).astype(BF16)
        s = lax.dot_general(qh, k, (((1,), (1,)), ((), ())), preferred_element_type=F32)
        s = jnp.where(kvalid, s + bias_ref[h], NEG)
        mx = jnp.max(s, axis=-1, keepdims=True)
        p = jnp.exp(s - mx)
        sums.append(jnp.sum(p, axis=-1, keepdims=True))
        maxs.append(mx)
        outs.append(jnp.dot(p.astype(BF16), v, preferred_element_type=F32))
      o = jnp.where(lane_lo, outs[0], outs[1])
      mx = jnp.where(lane_lo, maxs[0], maxs[1])
      sm = jnp.where(lane_lo, sums[0], sums[1])
      if g == 0:
        acc_ref[rows_q, :] = o
        m_ref[rows_q, :] = mx
        l_ref[rows_q, :] = sm
      else:
        m_old = m_ref[rows_q, :]
        m_new = jnp.maximum(m_old, mx)
        a_old = jnp.exp(m_old - m_new)
        a_new = jnp.exp(mx - m_new)
        acc_ref[rows_q, :] = acc_ref[rows_q, :] * a_old + o * a_new
        l_ref[rows_q, :] = l_ref[rows_q, :] * a_old + sm * a_new
        m_ref[rows_q, :] = m_new
      return carry

    lax.fori_loop(0, SEQ // ATT_BQ, block, 0)

  o_ref[...] = (acc_ref[...] / l_ref[...]).astype(BF16)


def _dil_attn(p, rel_bias, bkt, qg, kg, batch):
  base = 3 * HY_WIDTH // LANES
  per_kind = ATT_WIDTH // LANES
  per_group = ATT_OUT // LANES

  def col(kind, g):
    return pl.BlockSpec(
        (SEQ, LANES), lambda b, pr, tab: (b, base + kind * per_kind + g * per_group + pr))

  in_specs = [col(kind, g) for g in range(N_GROUPS) for kind in range(3)]
  in_specs += [pl.BlockSpec(bkt.shape, lambda b, pr, tab: (0, 0, 0)),
               pl.BlockSpec((1, LANES), lambda b, pr, tab: (0, 0)),
               pl.BlockSpec((1, LANES), lambda b, pr, tab: (0, 0))]
  grid_spec = pltpu.PrefetchScalarGridSpec(
      num_scalar_prefetch=1,
      grid=(batch, per_group),
      in_specs=in_specs,
      out_specs=pl.BlockSpec((SEQ, LANES), lambda b, pr, tab: (b, pr)),
      scratch_shapes=[pltpu.VMEM((SEQ, LANES), F32),
                      pltpu.VMEM((SEQ + 2 * ATT_PAD, LANES), F32),
                      pltpu.VMEM((SEQ + 2 * ATT_PAD, LANES), F32),
                      pltpu.VMEM((SEQ, LANES), F32), pltpu.VMEM((SEQ, LANES), F32),
                      pltpu.VMEM((SEQ, LANES), F32),
                      pltpu.VMEM((2, ATT_BQ, ATT_BK), F32)])
  return pl.pallas_call(
      _dil_attn_kernel,
      grid_spec=grid_spec,
      out_shape=jax.ShapeDtypeStruct((batch * SEQ, ATT_OUT), BF16),
      compiler_params=pltpu.CompilerParams(
          dimension_semantics=("arbitrary", "arbitrary"), vmem_limit_bytes=VMEM_LIMIT_BYTES),
      name="dil_attn",
  )(rel_bias, *([p] * 9), bkt, qg, kg)


def _filter_features():
  t = jnp.linspace(0.0, 1.0, SEQ, dtype=F32)[:, None]
  w = (2.0 * math.pi / SEQ) * jnp.arange(SEQ, dtype=F32)[:, None]
  f = jnp.linspace(1e-4, HY_POS_BANDS - 1, HY_POS_BANDS, dtype=F32)[None]
  z = jnp.concatenate([t, jnp.cos(f * w), -jnp.sin(f * w)], axis=-1)
  return jnp.pad(z, ((0, 0), (0, LANES - HY_EMB)))


def _decay_rates():
  max_decay = math.log(HY_DECAY_TARGET) / HY_FAST_DECAY_PCT
  min_decay = math.log(HY_DECAY_TARGET) / HY_SLOW_DECAY_PCT
  return jnp.abs(jnp.linspace(min_decay, max_decay, HY_WIDTH, dtype=F32))[None]


def kernel(x, rel_bias, ffn1_norm, ffn1_w_gate, ffn1_w_up, ffn1_w_down, mix_norm, w_in, w_gate, b_gate, hy_conv_w, hy_conv_b, hy_filt_w1, hy_filt_b1, hy_filt_w2, hy_filt_b2, hy_filt_w3, hy_skip, q_norm, k_norm, w_hy_proj, w_at_proj, w_out, ffn2_norm, ffn2_w_gate, ffn2_w_up, ffn2_w_down):
  batch, seq, d_model = x.shape
  assert (seq, d_model) == (SEQ, D_MODEL)
  depth = ffn1_norm.shape[0]
  mk, mki, md, mdi = _dft_constants()
  zfeat = _filter_features()
  deltas = _decay_rates()
  bkt = _bucket_maps()
  bf = lambda w: w.astype(BF16)
  row = lambda v: v.reshape(1, -1)

  xt = x.reshape(batch * seq, d_model)
  for l in range(depth):
    xt, p = _ffn_proj(xt, row(ffn1_norm[l]), bf(ffn1_w_gate[l]), bf(ffn1_w_up[l]),
                      bf(ffn1_w_down[l]), row(mix_norm[l]), bf(w_in[l]))
    w1 = jnp.pad(hy_filt_w1[l], ((0, LANES - HY_EMB), (0, 0)))
    hspec = _hy_filter(zfeat, w1, row(hy_filt_b1[l]), hy_filt_w2[l], row(hy_filt_b2[l]),
                       hy_filt_w3[l], deltas, mk, md)
    cw = jnp.concatenate([hy_conv_w[l], hy_conv_b[l][None]], axis=0)
    cw = cw.reshape(4, 3, HY_WIDTH).transpose(1, 0, 2)
    yhy = _hyena(p, cw, hy_skip[l], hspec, mk, mki, md, mdi, batch)
    yat = _dil_attn(p, rel_bias, bkt, row(jnp.tile(q_norm[l], 2)), row(jnp.tile(k_norm[l], 2)),
                    batch)
    xt = _merge_ffn(xt, yhy, yat, row(mix_norm[l]), bf(w_gate[l]), row(b_gate[l]),
                    bf(w_hy_proj[l]), bf(w_at_proj[l]), bf(w_out[l]), row(ffn2_norm[l]),
                    bf(ffn2_w_gate[l]), bf(ffn2_w_up[l]), bf(ffn2_w_down[l]))
  return xt.reshape(batch, seq, d_model)
```

```python
import functools
import math

import jax
import jax.numpy as jnp
from jax import lax
from jax.experimental import pallas as pl
from jax.experimental.pallas import tpu as pltpu

F32 = jnp.float32
BF16 = jnp.bfloat16

D_MODEL = 1024
SEQ = 4096
HEAD_DIM = 64
HY_WIDTH = 512
HY_ORDER = 2
HY_FILT_HIDDEN = 64
HY_POS_BANDS = 16
HY_EMB = 1 + 2 * HY_POS_BANDS
HY_FAST_DECAY_PCT = 0.3
HY_SLOW_DECAY_PCT = 1.5
HY_DECAY_TARGET = 1e-2
WINDOWS = (128, 512, 2048)
DILATIONS = (1, 4, 16)
N_GROUPS = 3
HEADS_PER_GROUP = 4
ATT_WIDTH = N_GROUPS * HEADS_PER_GROUP * HEAD_DIM
ATT_OUT = HEADS_PER_GROUP * HEAD_DIM
REL_BUCKETS = 32
REL_MAX_DISTANCE = 1024
D_FF = (8 * D_MODEL // 3) // 128 * 128
IN_WIDTH = 3 * HY_WIDTH + 3 * ATT_WIDTH
EPS = 1e-6
NEG = -1e30

SUBLANES = 8
LANES = 128
VMEM_LIMIT_BYTES = 60 * 1024 * 1024

DFT_N = 2 * SEQ
DFT_N2 = 128
DFT_N1 = SEQ // DFT_N2
DFT_F1 = 64
DFT_F2 = SEQ // DFT_F1
DFT_GROUPS = DFT_N2 // SUBLANES
KROWS = DFT_N1 * SUBLANES
UROWS = 2 * DFT_F1 * SUBLANES

N_SIDE = 64
ATT_BQ = 128
ATT_BK = ATT_BQ + 2 * N_SIDE
ATT_PAD = N_SIDE * max(DILATIONS)

FFN_CHUNK = 896
TOKEN_TILE = 256
HY_CB = 128
K_UNROLL = 2
D_UNROLL = 8
ATT_UNROLL = 4


def _const_spec(shape):
  nd = len(shape)
  return pl.BlockSpec(shape, lambda *_: (0,) * nd, pipeline_mode=pl.Buffered(1))


def _rms(x, g):
  return x * lax.rsqrt(jnp.mean(x * x, axis=-1, keepdims=True) + EPS) * g


def _swiglu_residual(x, g_ref, wg_ref, wu_ref, wd_ref):
  h = _rms(x, g_ref[...]).astype(BF16)
  acc = jnp.zeros_like(x)
  for c0 in range(0, D_FF, FFN_CHUNK):
    a = jnp.dot(h, wg_ref[:, c0:c0 + FFN_CHUNK], preferred_element_type=F32)
    u = jnp.dot(h, wu_ref[:, c0:c0 + FFN_CHUNK], preferred_element_type=F32)
    s = (a * jax.nn.sigmoid(a) * u).astype(BF16)
    acc = acc + jnp.dot(s, wd_ref[c0:c0 + FFN_CHUNK, :], preferred_element_type=F32)
  return x + 0.5 * acc


def _ffn_proj_kernel(x_ref, g1_ref, wg_ref, wu_ref, wd_ref, gm_ref, win_ref, x1_ref, p_ref):
  x1 = _swiglu_residual(x_ref[...], g1_ref, wg_ref, wu_ref, wd_ref)
  x1_ref[...] = x1
  h = _rms(x1, gm_ref[...]).astype(BF16)
  step = IN_WIDTH // 5
  for c0 in range(0, IN_WIDTH, step):
    p_ref[:, c0:c0 + step] = jnp.dot(
        h, win_ref[:, c0:c0 + step], preferred_element_type=F32).astype(BF16)


def _ffn_proj(x, g1, wg, wu, wd, gm, win):
  t = x.shape[0]
  tm = TOKEN_TILE
  tok = lambda w: pl.BlockSpec((tm, w), lambda i: (i, 0))
  return pl.pallas_call(
      _ffn_proj_kernel,
      grid=(t // tm,),
      in_specs=[tok(D_MODEL), _const_spec((1, D_MODEL)), _const_spec((D_MODEL, D_FF)),
                _const_spec((D_MODEL, D_FF)), _const_spec((D_FF, D_MODEL)),
                _const_spec((1, D_MODEL)), _const_spec((D_MODEL, IN_WIDTH))],
      out_specs=[tok(D_MODEL), tok(IN_WIDTH)],
      out_shape=[jax.ShapeDtypeStruct((t, D_MODEL), F32), jax.ShapeDtypeStruct((t, IN_WIDTH), BF16)],
      compiler_params=pltpu.CompilerParams(
          dimension_semantics=("arbitrary",), vmem_limit_bytes=VMEM_LIMIT_BYTES),
      name="ffn_proj",
  )(x, g1, wg, wu, wd, gm, win)


def _merge_ffn_kernel(x_ref, yhy_ref, yat_ref, gm_ref, wgate_ref, bgate_ref, why_ref, wat_ref,
                      wout_ref, g2_ref, wg_ref, wu_ref, wd_ref, o_ref):
  x = x_ref[...]
  h = _rms(x, gm_ref[...]).astype(BF16)
  gates = jax.nn.sigmoid(
      jnp.dot(h, wgate_ref[...], preferred_element_type=F32) + bgate_ref[...])
  a = jnp.dot(yhy_ref[...], why_ref[...], preferred_element_type=F32)
  b = jnp.dot(yat_ref[...], wat_ref[...], preferred_element_type=F32)
  y = gates[:, :D_MODEL] * a + gates[:, D_MODEL:] * b
  x2 = x + jnp.dot(y.astype(BF16), wout_ref[...], preferred_element_type=F32)
  o_ref[...] = _swiglu_residual(x2, g2_ref, wg_ref, wu_ref, wd_ref)


def _merge_ffn(x, yhy, yat, gm, wgate, bgate, why, wat, wout, g2, wg, wu, wd):
  t = x.shape[0]
  tm = TOKEN_TILE
  tok = lambda w: pl.BlockSpec((tm, w), lambda i: (i, 0))
  return pl.pallas_call(
      _merge_ffn_kernel,
      grid=(t // tm,),
      in_specs=[tok(D_MODEL), tok(HY_WIDTH), tok(ATT_OUT), _const_spec((1, D_MODEL)),
                _const_spec((D_MODEL, 2 * D_MODEL)), _const_spec((1, 2 * D_MODEL)),
                _const_spec((HY_WIDTH, D_MODEL)), _const_spec((ATT_OUT, D_MODEL)),
                _const_spec((D_MODEL, D_MODEL)), _const_spec((1, D_MODEL)),
                _const_spec((D_MODEL, D_FF)), _const_spec((D_MODEL, D_FF)),
                _const_spec((D_FF, D_MODEL))],
      out_specs=tok(D_MODEL),
      out_shape=jax.ShapeDtypeStruct((t, D_MODEL), F32),
      compiler_params=pltpu.CompilerParams(
          dimension_semantics=("arbitrary",), vmem_limit_bytes=VMEM_LIMIT_BYTES),
      name="merge_ffn",
  )(x, yhy, yat, gm, wgate, bgate, why, wat, wout, g2, wg, wu, wd)


def _dft_constants():
  two_pi = 2.0 * math.pi
  j = jnp.arange(DFT_GROUPS, dtype=jnp.int32)
  n1 = jnp.arange(DFT_N1, dtype=jnp.int32)
  s = jnp.arange(SUBLANES, dtype=jnp.int32)
  f1 = jnp.arange(DFT_F1, dtype=jnp.int32)
  n = DFT_N2 * n1[None, :, None] + SUBLANES * j[:, None, None] + s[None, None, :]
  ph = ((2 * f1 + 1)[None, :, None, None] * n[:, None, :, :]) % (2 * DFT_N)
  ang = ph.astype(F32) * (two_pi / (2 * DFT_N))
  c, sn = jnp.cos(ang), jnp.sin(ang)
  eye = jnp.eye(SUBLANES, dtype=F32)
  kr = jnp.einsum("jfns,st->jfsnt", c, eye)
  ki = jnp.einsum("jfns,st->jfsnt", -sn, eye)
  mk = jnp.stack([kr, ki], axis=1).reshape(DFT_GROUPS, UROWS, KROWS)
  scale = 2.0 / DFT_N
  gr = jnp.einsum("jfns,st->jntfs", c * scale, eye)
  gi = jnp.einsum("jfns,st->jntfs", -sn * scale, eye)
  mki = jnp.stack([gr, gi], axis=3).reshape(DFT_GROUPS, KROWS, UROWS)
  f2 = jnp.arange(DFT_F2, dtype=jnp.int32)
  n2 = jnp.arange(DFT_N2, dtype=jnp.int32)
  a2 = ((f2[:, None] * n2[None, :]) % DFT_N2).astype(F32) * (two_pi / DFT_N2)
  c2, s2 = jnp.cos(a2), jnp.sin(a2)
  md = jnp.concatenate([jnp.concatenate([c2, s2], axis=1),
                        jnp.concatenate([-s2, c2], axis=1)], axis=0)
  mdi = jnp.concatenate([jnp.concatenate([c2.T, -s2.T], axis=1),
                         jnp.concatenate([s2.T, c2.T], axis=1)], axis=0)
  return mk.astype(BF16), mki.astype(BF16), md.astype(BF16), mdi.astype(BF16)


def _stage_k_forward(z_ref, mk_ref, u_ref):
  def body(j, carry):
    tiles = [z_ref[pl.ds(pl.multiple_of(DFT_N2 * n1 + SUBLANES * j, SUBLANES), SUBLANES), :]
             for n1 in range(DFT_N1)]
    rhs = jnp.concatenate(tiles, axis=0).astype(BF16)
    u_ref[j] = jnp.dot(mk_ref[j], rhs, preferred_element_type=F32)
    return carry

  lax.fori_loop(0, DFT_GROUPS, body, 0, unroll=K_UNROLL)


def _gather_f1(u_ref, f1):
  tiles = []
  for ri in range(2):
    for j in range(DFT_GROUPS):
      row = pl.multiple_of(ri * DFT_F1 * SUBLANES + f1 * SUBLANES, SUBLANES)
      tiles.append(u_ref[j, pl.ds(row, SUBLANES), :])
  return jnp.concatenate(tiles, axis=0)


def _stage_k_inverse(u_ref, mki_ref, y_ref):
  def body(j, carry):
    y = jnp.dot(mki_ref[j], u_ref[j].astype(BF16), preferred_element_type=F32)
    for n1 in range(DFT_N1):
      y_ref[pl.ds(pl.multiple_of(DFT_N2 * n1 + SUBLANES * j, SUBLANES), SUBLANES), :] = (
          y[SUBLANES * n1:SUBLANES * (n1 + 1), :])
    return carry

  lax.fori_loop(0, DFT_GROUPS, body, 0, unroll=K_UNROLL)


def _long_conv(z_ref, y_ref, u_ref, h_ref, order, mk_ref, mki_ref, md_ref, mdi_ref):
  _stage_k_forward(z_ref, mk_ref, u_ref)

  def body(f1, carry):
    x = jnp.dot(md_ref[...], _gather_f1(u_ref, f1).astype(BF16), preferred_element_type=F32)
    xr, xi = x[:DFT_F2], x[DFT_F2:]
    hr = h_ref[order, f1, pl.ds(0, DFT_F2), :]
    hi = h_ref[order, f1, pl.ds(DFT_F2, DFT_F2), :]
    y = jnp.concatenate([xr * hr - xi * hi, xr * hi + xi * hr], axis=0).astype(BF16)
    w = jnp.dot(mdi_ref[...], y, preferred_element_type=F32)
    k = 0
    for ri in range(2):
      for j in range(DFT_GROUPS):
        row = pl.multiple_of(ri * DFT_F1 * SUBLANES + f1 * SUBLANES, SUBLANES)
        u_ref[j, pl.ds(row, SUBLANES), :] = w[SUBLANES * k:SUBLANES * (k + 1), :]
        k += 1
    return carry

  lax.fori_loop(0, DFT_F1, body, 0, unroll=D_UNROLL)
  _stage_k_inverse(u_ref, mki_ref, y_ref)


def _hy_filter_kernel(z_ref, w1_ref, b1_ref, w2_ref, b2_ref, w3f_ref, w3b_ref, dl_ref,
                      mk_ref, md_ref, h_ref, k_ref, u_ref):
  hp = lax.Precision.HIGHEST
  z = z_ref[...]
  hid = jnp.sin(jnp.dot(z, w1_ref[...], precision=hp, preferred_element_type=F32) + b1_ref[...])
  hid = jnp.sin(jnp.dot(hid, w2_ref[...], precision=hp, preferred_element_type=F32) + b2_ref[...])
  decay = jnp.exp(-z[:, 0:1] * dl_ref[...])
  kf = jnp.dot(hid, w3f_ref[...], precision=hp, preferred_element_type=F32) * decay
  kb = jnp.dot(hid, w3b_ref[...], precision=hp, preferred_element_type=F32) * decay
  row = lax.broadcasted_iota(jnp.int32, kb.shape, 0)
  kb = jnp.where(row == 0, 0.0, kb)
  rs = lax.rsqrt(jnp.sum(kf * kf + kb * kb, axis=0, keepdims=True) + EPS)
  c = kf.shape[1]
  k_ref[:, :c] = kf
  k_ref[:, c:] = kb
  _stage_k_forward(k_ref, mk_ref, u_ref)

  def body(f1, carry):
    x = jnp.dot(md_ref[...], _gather_f1(u_ref, f1).astype(BF16), preferred_element_type=F32)
    h_ref[0, f1, pl.ds(0, DFT_F2), :] = (x[:DFT_F2, :c] + x[:DFT_F2, c:]) * rs
    h_ref[0, f1, pl.ds(DFT_F2, DFT_F2), :] = (x[DFT_F2:, :c] - x[DFT_F2:, c:]) * rs
    return carry

  lax.fori_loop(0, DFT_F1, body, 0, unroll=D_UNROLL)


def _hy_filter(zfeat, w1, b1, w2, b2, w3, deltas, mk, md):
  c = HY_CB
  ncb = HY_WIDTH // c
  blocks_per_dir = HY_WIDTH // c
  return pl.pallas_call(
      _hy_filter_kernel,
      grid=(HY_ORDER, ncb),
      in_specs=[_const_spec(zfeat.shape), _const_spec(w1.shape), _const_spec(b1.shape),
                _const_spec(w2.shape), _const_spec(b2.shape),
                pl.BlockSpec((HY_FILT_HIDDEN, c), lambda o, cb: (0, o * 2 * blocks_per_dir + cb)),
                pl.BlockSpec((HY_FILT_HIDDEN, c),
                             lambda o, cb: (0, o * 2 * blocks_per_dir + blocks_per_dir + cb)),
                pl.BlockSpec((1, c), lambda o, cb: (0, cb)),
                _const_spec(mk.shape), _const_spec(md.shape)],
      out_specs=pl.BlockSpec((1, DFT_F1, 2 * DFT_F2, c), lambda o, cb: (o, 0, 0, cb)),
      out_shape=jax.ShapeDtypeStruct((HY_ORDER, DFT_F1, 2 * DFT_F2, HY_WIDTH), F32),
      scratch_shapes=[pltpu.VMEM((SEQ, 2 * c), F32),
                      pltpu.VMEM((DFT_GROUPS, UROWS, 2 * c), F32)],
      compiler_params=pltpu.CompilerParams(
          dimension_semantics=("arbitrary", "arbitrary"), vmem_limit_bytes=VMEM_LIMIT_BYTES),
      name="hy_filter",
  )(zfeat, w1, b1, w2, b2, w3, w3, deltas, mk, md)


def _conv3(p_ref, cw_ref, k):
  u = p_ref[...].astype(F32)
  rows = u.shape[0]
  row = lax.broadcasted_iota(jnp.int32, u.shape, 0)
  prev = jnp.where(row == 0, 0.0, pltpu.roll(u, 1, 0))
  nxt = jnp.where(row == rows - 1, 0.0, pltpu.roll(u, rows - 1, 0))
  w = cw_ref[k]
  return prev * w[0:1] + u * w[1:2] + nxt * w[2:3] + w[3:4]


def _hyena_kernel(pv_ref, px1_ref, px2_ref, cw_ref, skip_ref, h_ref, mk_ref, mki_ref, md_ref,
                  mdi_ref, o_ref, z_ref, y_ref, u_ref):
  z_ref[...] = _conv3(pv_ref, cw_ref, 0)
  gates = (px1_ref, px2_ref)
  for order in range(HY_ORDER):
    _long_conv(z_ref, y_ref, u_ref, h_ref, order, mk_ref, mki_ref, md_ref, mdi_ref)
    gate = _conv3(gates[order], cw_ref, 1 + order)
    z = gate * (y_ref[...] + skip_ref[order:order + 1, :] * z_ref[...])
    if order + 1 < HY_ORDER:
      z_ref[...] = z
    else:
      o_ref[...] = z.astype(BF16)


def _hyena(p, cw, skip, hspec, mk, mki, md, mdi, batch):
  c = HY_CB
  ncb = HY_WIDTH // c
  col = lambda k: pl.BlockSpec((SEQ, c), lambda cb, b: (b, k * ncb + cb))
  return pl.pallas_call(
      _hyena_kernel,
      grid=(ncb, batch),
      in_specs=[col(0), col(1), col(2),
                pl.BlockSpec((3, 4, c), lambda cb, b: (0, 0, cb)),
                pl.BlockSpec((HY_ORDER, c), lambda cb, b: (0, cb)),
                pl.BlockSpec((HY_ORDER, DFT_F1, 2 * DFT_F2, c), lambda cb, b: (0, 0, 0, cb),
                             pipeline_mode=pl.Buffered(1)),
                _const_spec(mk.shape), _const_spec(mki.shape), _const_spec(md.shape),
                _const_spec(mdi.shape)],
      out_specs=pl.BlockSpec((SEQ, c), lambda cb, b: (b, cb)),
      out_shape=jax.ShapeDtypeStruct((batch * SEQ, HY_WIDTH), BF16),
      scratch_shapes=[pltpu.VMEM((SEQ, c), F32), pltpu.VMEM((SEQ, c), F32),
                      pltpu.VMEM((DFT_GROUPS, UROWS, c), F32)],
      compiler_params=pltpu.CompilerParams(
          dimension_semantics=("arbitrary", "arbitrary"), vmem_limit_bytes=VMEM_LIMIT_BYTES),
      name="hyena",
  )(p, p, p, cw, skip, hspec, mk, mki, md, mdi)


def _t5_bucket(rel):
  half = REL_BUCKETS // 2
  exact = half // 2
  ret = jnp.where(rel > 0, half, 0)
  n = jnp.abs(rel)
  nf = jnp.maximum(n, 1).astype(F32)
  large = exact + (jnp.log(nf / exact) / math.log(REL_MAX_DISTANCE / exact) * (half - exact)).astype(jnp.int32)
  large = jnp.minimum(large, half - 1)
  return ret + jnp.where(n < exact, n, large)


def _bucket_maps():
  qi = jnp.arange(ATT_BQ)[:, None]
  ki = jnp.arange(ATT_BK)[None, :]
  rel = ki - N_SIDE - qi
  maps = [jnp.where(jnp.abs(rel) <= N_SIDE, _t5_bucket(rel * d), -1) for d in DILATIONS]
  return jnp.stack(maps).astype(jnp.int32)


def _head_norm(x, gain, lane_lo):
  sq = x * x
  s_lo = jnp.sum(jnp.where(lane_lo, sq, 0.0), axis=-1, keepdims=True)
  s_all = jnp.sum(sq, axis=-1, keepdims=True)
  ms = jnp.where(lane_lo, s_lo, s_all - s_lo) * (1.0 / HEAD_DIM)
  return x * lax.rsqrt(ms + EPS) * gain


def _dil_attn_kernel(tab_ref, *refs):
  qkv_refs = refs[:9]
  bkt_ref, qg_ref, kg_ref, o_ref = refs[9:13]
  qs_ref, ks_ref, vs_ref, acc_ref, m_ref, l_ref, bias_ref = refs[13:]
  pair = pl.program_id(1)
  lane_lo = lax.broadcasted_iota(jnp.int32, (1, LANES), 1) < HEAD_DIM
  key_idx = lax.broadcasted_iota(jnp.int32, (1, ATT_BK), 1)

  for g, d in enumerate(DILATIONS):
    q_ref, k_ref, v_ref = qkv_refs[3 * g:3 * g + 3]
    m_sub = SEQ // d
    pad = N_SIDE * d
    bkt = bkt_ref[g]
    for h in range(2):
      col = g * HEADS_PER_GROUP + 2 * pair + h
      b = jnp.where(bkt < 0, NEG, 0.0)
      for t in range(REL_BUCKETS):
        b = b + jnp.where(bkt == t, tab_ref[t, col], 0.0)
      bias_ref[h] = b
    qs_ref[...] = _head_norm(q_ref[...].astype(F32), qg_ref[...], lane_lo) * (HEAD_DIM ** -0.5)
    zeros = jnp.zeros((pad, LANES), F32)
    ks_ref[pl.ds(0, pad), :] = zeros
    ks_ref[pl.ds(pad + SEQ, pad), :] = zeros
    vs_ref[pl.ds(0, pad), :] = zeros
    vs_ref[pl.ds(pad + SEQ, pad), :] = zeros
    ks_ref[pl.ds(pad, SEQ), :] = _head_norm(k_ref[...].astype(F32), kg_ref[...], lane_lo)
    vs_ref[pl.ds(pad, SEQ), :] = v_ref[...].astype(F32)

    def block(t, carry, d=d, m_sub=m_sub, g=g):
      r = t % d
      i = t // d
      start = r + d * ATT_BQ * i
      if d == 1:
        rows_q = pl.ds(start, ATT_BQ)
        rows_k = pl.ds(start, ATT_BK)
      else:
        rows_q = pl.ds(start, ATT_BQ, stride=d)
        rows_k = pl.ds(start, ATT_BK, stride=d)
      q = qs_ref[rows_q, :]
      k = ks_ref[rows_k, :].astype(BF16)
      v = vs_ref[rows_k, :].astype(BF16)
      kpos = ATT_BQ * i - N_SIDE + key_idx
      kvalid = (kpos >= 0) & (kpos < m_sub)
      outs, maxs, sums = [], [], []
      for h in range(2):
        qh = jnp.where(lane_lo if h == 0 else jnp.logical_not(lane_lo), q, 0.0).astype(BF16)
        s = lax.dot_general(qh, k, (((1,), (1,)), ((), ())), preferred_element_type=F32)
        s = jnp.where(kvalid, s + bias_ref[h], NEG)
        mx = jnp.max(s, axis=-1, keepdims=True)
        p = jnp.exp(s - mx)
        sums.append(jnp.sum(p, axis=-1, keepdims=True))
        maxs.append(mx)
        outs.append(jnp.dot(p.astype(BF16), v, preferred_element_type=F32))
      o = jnp.where(lane_lo, outs[0], outs[1])
      mx = jnp.where(lane_lo, maxs[0], maxs[1])
      sm = jnp.where(lane_lo, sums[0], sums[1])
      if g == 0:
        acc_ref[rows_q, :] = o
        m_ref[rows_q, :] = mx
        l_ref[rows_q, :] = sm
      else:
        m_old = m_ref[rows_q, :]
        m_new = jnp.maximum(m_old, mx)
        a_old = jnp.exp(m_old - m_new)
        a_new = jnp.exp(mx - m_new)
        acc_ref[rows_q, :] = acc_ref[rows_q, :] * a_old + o * a_new
        l_ref[rows_q, :] = l_ref[rows_q, :] * a_old + sm * a_new
        m_ref[rows_q, :] = m_new
      return carry

    lax.fori_loop(0, SEQ // ATT_BQ, block, 0, unroll=ATT_UNROLL)

  o_ref[...] = (acc_ref[...] / l_ref[...]).astype(BF16)


def _dil_attn(p, rel_bias, bkt, qg, kg, batch):
  base = 3 * HY_WIDTH // LANES
  per_kind = ATT_WIDTH // LANES
  per_group = ATT_OUT // LANES

  def col(kind, g):
    return pl.BlockSpec(
        (SEQ, LANES), lambda b, pr, tab: (b, base + kind * per_kind + g * per_group + pr))

  in_specs = [col(kind, g) for g in range(N_GROUPS) for kind in range(3)]
  in_specs += [pl.BlockSpec(bkt.shape, lambda b, pr, tab: (0, 0, 0)),
               pl.BlockSpec((1, LANES), lambda b, pr, tab: (0, 0)),
               pl.BlockSpec((1, LANES), lambda b, pr, tab: (0, 0))]
  grid_spec = pltpu.PrefetchScalarGridSpec(
      num_scalar_prefetch=1,
      grid=(batch, per_group),
      in_specs=in_specs,
      out_specs=pl.BlockSpec((SEQ, LANES), lambda b, pr, tab: (b, pr)),
      scratch_shapes=[pltpu.VMEM((SEQ, LANES), F32),
                      pltpu.VMEM((SEQ + 2 * ATT_PAD, LANES), F32),
                      pltpu.VMEM((SEQ + 2 * ATT_PAD, LANES), F32),
                      pltpu.VMEM((SEQ, LANES), F32), pltpu.VMEM((SEQ, LANES), F32),
                      pltpu.VMEM((SEQ, LANES), F32),
                      pltpu.VMEM((2, ATT_BQ, ATT_BK), F32)])
  return pl.pallas_call(
      _dil_attn_kernel,
      grid_spec=grid_spec,
      out_shape=jax.ShapeDtypeStruct((batch * SEQ, ATT_OUT), BF16),
      compiler_params=pltpu.CompilerParams(
          dimension_semantics=("arbitrary", "arbitrary"), vmem_limit_bytes=VMEM_LIMIT_BYTES),
      name="dil_attn",
  )(rel_bias, *([p] * 9), bkt, qg, kg)


def _filter_features():
  t = jnp.linspace(0.0, 1.0, SEQ, dtype=F32)[:, None]
  w = (2.0 * math.pi / SEQ) * jnp.arange(SEQ, dtype=F32)[:, None]
  f = jnp.linspace(1e-4, HY_POS_BANDS - 1, HY_POS_BANDS, dtype=F32)[None]
  z = jnp.concatenate([t, jnp.cos(f * w), -jnp.sin(f * w)], axis=-1)
  return jnp.pad(z, ((0, 0), (0, LANES - HY_EMB)))


def _decay_rates():
  max_decay = math.log(HY_DECAY_TARGET) / HY_FAST_DECAY_PCT
  min_decay = math.log(HY_DECAY_TARGET) / HY_SLOW_DECAY_PCT
  return jnp.abs(jnp.linspace(min_decay, max_decay, HY_WIDTH, dtype=F32))[None]


def kernel(x, rel_bias, ffn1_norm, ffn1_w_gate, ffn1_w_up, ffn1_w_down, mix_norm, w_in, w_gate, b_gate, hy_conv_w, hy_conv_b, hy_filt_w1, hy_filt_b1, hy_filt_w2, hy_filt_b2, hy_filt_w3, hy_skip, q_norm, k_norm, w_hy_proj, w_at_proj, w_out, ffn2_norm, ffn2_w_gate, ffn2_w_up, ffn2_w_down):
  batch, seq, d_model = x.shape
  assert (seq, d_model) == (SEQ, D_MODEL)
  depth = ffn1_norm.shape[0]
  mk, mki, md, mdi = _dft_constants()
  zfeat = _filter_features()
  deltas = _decay_rates()
  bkt = _bucket_maps()
  bf = lambda w: w.astype(BF16)
  row = lambda v: v.reshape(1, -1)

  xt = x.reshape(batch * seq, d_model)
  for l in range(depth):
    xt, p = _ffn_proj(xt, row(ffn1_norm[l]), bf(ffn1_w_gate[l]), bf(ffn1_w_up[l]),
                      bf(ffn1_w_down[l]), row(mix_norm[l]), bf(w_in[l]))
    w1 = jnp.pad(hy_filt_w1[l], ((0, LANES - HY_EMB), (0, 0)))
    hspec = _hy_filter(zfeat, w1, row(hy_filt_b1[l]), hy_filt_w2[l], row(hy_filt_b2[l]),
                       hy_filt_w3[l], deltas, mk, md)
    cw = jnp.concatenate([hy_conv_w[l], hy_conv_b[l][None]], axis=0)
    cw = cw.reshape(4, 3, HY_WIDTH).transpose(1, 0, 2)
    yhy = _hyena(p, cw, hy_skip[l], hspec, mk, mki, md, mdi, batch)
    yat = _dil_attn(p, rel_bias, bkt, row(jnp.tile(q_norm[l], 2)), row(jnp.tile(k_norm[l], 2)),
                    batch)
    xt = _merge_ffn(xt, yhy, yat, row(mix_norm[l]), bf(w_gate[l]), row(b_gate[l]),
                    bf(w_hy_proj[l]), bf(w_at_proj[l]), bf(w_out[l]), row(ffn2_norm[l]),
                    bf(ffn2_w_gate[l]), bf(ffn2_w_up[l]), bf(ffn2_w_down[l]))
  return xt.reshape(batch, seq, d_model)
```

```python
import functools
import math

import jax
import jax.numpy as jnp
import numpy as np
from jax import lax
from jax.experimental import pallas as pl
from jax.experimental.pallas import tpu as pltpu

F32 = jnp.float32
BF16 = jnp.bfloat16

D_MODEL = 1024
SEQ = 4096
HEAD_DIM = 64
HY_WIDTH = 512
HY_ORDER = 2
HY_FILT_HIDDEN = 64
HY_POS_BANDS = 16
HY_EMB = 1 + 2 * HY_POS_BANDS
HY_FAST_DECAY_PCT = 0.3
HY_SLOW_DECAY_PCT = 1.5
HY_DECAY_TARGET = 1e-2
WINDOWS = (128, 512, 2048)
DILATIONS = (1, 4, 16)
N_GROUPS = 3
HEADS_PER_GROUP = 4
ATT_WIDTH = N_GROUPS * HEADS_PER_GROUP * HEAD_DIM
ATT_OUT = HEADS_PER_GROUP * HEAD_DIM
REL_BUCKETS = 32
REL_MAX_DISTANCE = 1024
D_FF = (8 * D_MODEL // 3) // 128 * 128
IN_WIDTH = 3 * HY_WIDTH + 3 * ATT_WIDTH
EPS = 1e-6
NEG = -1e30

SUBLANES = 8
LANES = 128
BF16_ROWS = 16
VMEM_LIMIT_BYTES = 60 * 1024 * 1024

DFT_N = 2 * SEQ
DFT_N2 = 128
DFT_N1 = SEQ // DFT_N2
DFT_F1 = 64
DFT_F2 = SEQ // DFT_F1
DFT_GROUPS = DFT_N2 // SUBLANES
KROWS = DFT_N1 * SUBLANES
UROWS = 2 * DFT_F1 * SUBLANES

N_SIDE = 64
ATT_BQ = 128
ATT_BK = ATT_BQ + 2 * N_SIDE
ATT_PAD = N_SIDE * max(DILATIONS)

FFN_CHUNKS = ((0, 1280), (1280, D_FF))
TOKEN_TILE = 512
HY_CB = 128
K_UNROLL = 2
D_UNROLL = 16
ATT_UNROLL = 4


def _const_spec(shape):
  nd = len(shape)
  return pl.BlockSpec(shape, lambda *_: (0,) * nd, pipeline_mode=pl.Buffered(1))


def _rms(x, g):
  return x * lax.rsqrt(jnp.mean(x * x, axis=-1, keepdims=True) + EPS) * g


def _swiglu_residual(x, g_ref, wg_ref, wu_ref, wd_ref):
  h = _rms(x, g_ref[...]).astype(BF16)
  acc = jnp.zeros_like(x)
  for c0, c1 in FFN_CHUNKS:
    a = jnp.dot(h, wg_ref[:, c0:c1], preferred_element_type=F32)
    u = jnp.dot(h, wu_ref[:, c0:c1], preferred_element_type=F32)
    s = (a * jax.nn.sigmoid(a) * u).astype(BF16)
    acc = acc + jnp.dot(s, wd_ref[c0:c1, :], preferred_element_type=F32)
  return x + 0.5 * acc


def _ffn_proj_kernel(x_ref, g1_ref, wg_ref, wu_ref, wd_ref, gm_ref, win_ref, x1_ref, p_ref):
  x1 = _swiglu_residual(x_ref[...], g1_ref, wg_ref, wu_ref, wd_ref)
  x1_ref[...] = x1
  h = _rms(x1, gm_ref[...]).astype(BF16)
  step = IN_WIDTH // 5
  for c0 in range(0, IN_WIDTH, step):
    p_ref[:, c0:c0 + step] = jnp.dot(
        h, win_ref[:, c0:c0 + step], preferred_element_type=F32).astype(BF16)


def _ffn_proj(x, g1, wg, wu, wd, gm, win):
  t = x.shape[0]
  tm = TOKEN_TILE
  tok = lambda w: pl.BlockSpec((tm, w), lambda i: (i, 0))
  return pl.pallas_call(
      _ffn_proj_kernel,
      grid=(t // tm,),
      in_specs=[tok(D_MODEL), _const_spec((1, D_MODEL)), _const_spec((D_MODEL, D_FF)),
                _const_spec((D_MODEL, D_FF)), _const_spec((D_FF, D_MODEL)),
                _const_spec((1, D_MODEL)), _const_spec((D_MODEL, IN_WIDTH))],
      out_specs=[tok(D_MODEL), tok(IN_WIDTH)],
      out_shape=[jax.ShapeDtypeStruct((t, D_MODEL), F32), jax.ShapeDtypeStruct((t, IN_WIDTH), BF16)],
      compiler_params=pltpu.CompilerParams(
          dimension_semantics=("arbitrary",), vmem_limit_bytes=VMEM_LIMIT_BYTES),
      name="ffn_proj",
  )(x, g1, wg, wu, wd, gm, win)


def _merge_ffn_kernel(x_ref, yhy_ref, yat_ref, gm_ref, wgate_ref, bgate_ref, why_ref, wat_ref,
                      wout_ref, g2_ref, wg_ref, wu_ref, wd_ref, o_ref):
  x = x_ref[...]
  h = _rms(x, gm_ref[...]).astype(BF16)
  gates = jax.nn.sigmoid(
      jnp.dot(h, wgate_ref[...], preferred_element_type=F32) + bgate_ref[...])
  a = jnp.dot(yhy_ref[...], why_ref[...], preferred_element_type=F32)
  b = jnp.dot(yat_ref[...], wat_ref[...], preferred_element_type=F32)
  y = gates[:, :D_MODEL] * a + gates[:, D_MODEL:] * b
  x2 = x + jnp.dot(y.astype(BF16), wout_ref[...], preferred_element_type=F32)
  o_ref[...] = _swiglu_residual(x2, g2_ref, wg_ref, wu_ref, wd_ref)


def _merge_ffn(x, yhy, yat, gm, wgate, bgate, why, wat, wout, g2, wg, wu, wd):
  t = x.shape[0]
  tm = TOKEN_TILE
  tok = lambda w: pl.BlockSpec((tm, w), lambda i: (i, 0))
  return pl.pallas_call(
      _merge_ffn_kernel,
      grid=(t // tm,),
      in_specs=[tok(D_MODEL), tok(HY_WIDTH), tok(ATT_OUT), _const_spec((1, D_MODEL)),
                _const_spec((D_MODEL, 2 * D_MODEL)), _const_spec((1, 2 * D_MODEL)),
                _const_spec((HY_WIDTH, D_MODEL)), _const_spec((ATT_OUT, D_MODEL)),
                _const_spec((D_MODEL, D_MODEL)), _const_spec((1, D_MODEL)),
                _const_spec((D_MODEL, D_FF)), _const_spec((D_MODEL, D_FF)),
                _const_spec((D_FF, D_MODEL))],
      out_specs=tok(D_MODEL),
      out_shape=jax.ShapeDtypeStruct((t, D_MODEL), F32),
      compiler_params=pltpu.CompilerParams(
          dimension_semantics=("arbitrary",), vmem_limit_bytes=VMEM_LIMIT_BYTES),
      name="merge_ffn",
  )(x, yhy, yat, gm, wgate, bgate, why, wat, wout, g2, wg, wu, wd)


@functools.lru_cache(maxsize=None)
def _dft_constants_host():
  f1 = np.arange(DFT_F1)
  n1 = np.arange(DFT_N1)
  f2 = np.arange(DFT_F2)
  n2 = np.arange(DFT_N2)
  eye = np.eye(SUBLANES)
  a_ang = 2.0 * np.pi * (((2 * f1 + 1)[:, None] * n1[None, :]) % DFT_N2) / DFT_N2
  ar, ai = np.cos(a_ang), -np.sin(a_ang)
  mk = np.stack([np.einsum("fn,st->fsnt", ar, eye), np.einsum("fn,st->fsnt", ai, eye)], axis=1)
  mk = mk.reshape(UROWS, KROWS)
  scale = 2.0 / DFT_N
  mki = np.stack([np.einsum("fn,st->ntfs", ar * scale, eye),
                  np.einsum("fn,st->ntfs", ai * scale, eye)], axis=3)
  mki = mki.reshape(KROWS, UROWS)
  ph = (n2[None, None, :] * (DFT_N2 * f2[None, :, None] + 2 * f1[:, None, None] + 1)) % (2 * DFT_N)
  d_ang = 2.0 * np.pi * ph / (2 * DFT_N)
  dr = np.cos(d_ang).reshape(DFT_F1, DFT_F2, DFT_GROUPS, SUBLANES)
  di = -np.sin(d_ang).reshape(DFT_F1, DFT_F2, DFT_GROUPS, SUBLANES)
  top = np.stack([dr, -di], axis=3)
  bot = np.stack([di, dr], axis=3)
  md = np.stack([top, bot], axis=1).reshape(DFT_F1, 2 * DFT_F2, 2 * DFT_N2)
  drt = dr.transpose(0, 2, 3, 1)
  dit = di.transpose(0, 2, 3, 1)
  re_rows = np.stack([drt, dit], axis=3)
  im_rows = np.stack([-dit, drt], axis=3)
  mdi = np.stack([re_rows, im_rows], axis=2).reshape(DFT_F1, 2 * DFT_N2, 2 * DFT_F2)
  return tuple(np.asarray(m, np.float32) for m in (mk, mki, md, mdi))


def _dft_constants():
  return tuple(jnp.asarray(m, dtype=BF16) for m in _dft_constants_host())


def _grouped_loop(count, group, body):
  def outer(i, carry):
    base = i * group
    for k in range(group):
      body(base + k, base, k)
    return carry

  lax.fori_loop(0, count // group, outer, 0)


def _tile_rows(base, scale, offset, size):
  return pl.ds(pl.multiple_of(base * scale, size) + offset, size)


def _stage_k_forward(z_ref, mk_ref, u_ref):
  def body(j, base, k):
    tiles = [z_ref[_tile_rows(base, SUBLANES, DFT_N2 * n1 + SUBLANES * k, SUBLANES), :]
             for n1 in range(DFT_N1)]
    rhs = jnp.concatenate(tiles, axis=0).astype(BF16)
    u_ref[j] = jnp.dot(mk_ref[...], rhs, preferred_element_type=F32).astype(BF16)

  _grouped_loop(DFT_GROUPS, K_UNROLL, body)


def _gather_f1(u_ref, base, k):
  rows = _tile_rows(base, BF16_ROWS, BF16_ROWS * k, BF16_ROWS)
  return jnp.concatenate([u_ref[j, rows, :] for j in range(DFT_GROUPS)], axis=0)


def _scatter_f1(u_ref, base, k, w):
  rows = _tile_rows(base, BF16_ROWS, BF16_ROWS * k, BF16_ROWS)
  wb = w.astype(BF16)
  for j in range(DFT_GROUPS):
    u_ref[j, rows, :] = wb[BF16_ROWS * j:BF16_ROWS * (j + 1), :]


def _stage_k_inverse(u_ref, mki_ref, emit):
  def body(j, base, k):
    y = jnp.dot(mki_ref[...], u_ref[j], preferred_element_type=F32)
    for n1 in range(DFT_N1):
      rows = _tile_rows(base, SUBLANES, DFT_N2 * n1 + SUBLANES * k, SUBLANES)
      emit(rows, y[SUBLANES * n1:SUBLANES * (n1 + 1), :])

  _grouped_loop(DFT_GROUPS, K_UNROLL, body)


def _hy_filter_kernel(z_ref, w1_ref, b1_ref, w2_ref, b2_ref, w3f_ref, w3b_ref, dl_ref,
                      mk_ref, md_ref, h_ref, hid_ref, k_ref, u_ref):
  hp = lax.Precision.HIGHEST

  @pl.when((pl.program_id(0) == 0) & (pl.program_id(1) == 0))
  def _():
    hid = jnp.sin(
        jnp.dot(z_ref[...], w1_ref[...], precision=hp, preferred_element_type=F32) + b1_ref[...])
    hid_ref[...] = jnp.sin(
        jnp.dot(hid, w2_ref[...], precision=hp, preferred_element_type=F32) + b2_ref[...])

  hid = hid_ref[...]
  decay = jnp.exp(-z_ref[:, 0:1] * dl_ref[...])
  kf = jnp.dot(hid, w3f_ref[...], precision=hp, preferred_element_type=F32) * decay
  kb = jnp.dot(hid, w3b_ref[...], precision=hp, preferred_element_type=F32) * decay
  row = lax.broadcasted_iota(jnp.int32, kb.shape, 0)
  kb = jnp.where(row == 0, 0.0, kb)
  rs = lax.rsqrt(jnp.sum(kf * kf + kb * kb, axis=0, keepdims=True) + EPS)
  c = kf.shape[1]
  k_ref[:, :c] = kf
  k_ref[:, c:] = kb
  _stage_k_forward(k_ref, mk_ref, u_ref)

  def body(f1, base, k):
    x = jnp.dot(md_ref[f1], _gather_f1(u_ref, base, k), preferred_element_type=F32)
    h_ref[0, f1, pl.ds(0, DFT_F2), :] = (x[:DFT_F2, :c] + x[:DFT_F2, c:]) * rs
    h_ref[0, f1, pl.ds(DFT_F2, DFT_F2), :] = (x[DFT_F2:, :c] - x[DFT_F2:, c:]) * rs

  _grouped_loop(DFT_F1, D_UNROLL, body)


def _hy_filter(zfeat, w1, b1, w2, b2, w3, deltas, mk, md):
  c = HY_CB
  ncb = HY_WIDTH // c
  return pl.pallas_call(
      _hy_filter_kernel,
      grid=(HY_ORDER, ncb),
      in_specs=[_const_spec(zfeat.shape), _const_spec(w1.shape), _const_spec(b1.shape),
                _const_spec(w2.shape), _const_spec(b2.shape),
                pl.BlockSpec((HY_FILT_HIDDEN, c), lambda o, cb: (0, o * 2 * ncb + cb)),
                pl.BlockSpec((HY_FILT_HIDDEN, c), lambda o, cb: (0, o * 2 * ncb + ncb + cb)),
                pl.BlockSpec((1, c), lambda o, cb: (0, cb)),
                _const_spec(mk.shape), _const_spec(md.shape)],
      out_specs=pl.BlockSpec((1, DFT_F1, 2 * DFT_F2, c), lambda o, cb: (o, 0, 0, cb)),
      out_shape=jax.ShapeDtypeStruct((HY_ORDER, DFT_F1, 2 * DFT_F2, HY_WIDTH), F32),
      scratch_shapes=[pltpu.VMEM((SEQ, HY_FILT_HIDDEN), F32),
                      pltpu.VMEM((SEQ, 2 * c), F32),
                      pltpu.VMEM((DFT_GROUPS, UROWS, 2 * c), BF16)],
      compiler_params=pltpu.CompilerParams(
          dimension_semantics=("arbitrary", "arbitrary"), vmem_limit_bytes=VMEM_LIMIT_BYTES),
      name="hy_filter",
  )(zfeat, w1, b1, w2, b2, w3, w3, deltas, mk, md)


def _conv3(p_ref, half, cw_ref, k):
  u = p_ref[pl.ds(half * SEQ, SEQ), :].astype(F32)
  row = lax.broadcasted_iota(jnp.int32, u.shape, 0)
  prev = jnp.where(row == 0, 0.0, pltpu.roll(u, 1, 0))
  nxt = jnp.where(row == SEQ - 1, 0.0, pltpu.roll(u, SEQ - 1, 0))
  w = cw_ref[k]
  return prev * w[0:1] + u * w[1:2] + nxt * w[2:3] + w[3:4]


def _hyena_kernel(pv_ref, pa_ref, pb_ref, cw_ref, skip_ref, h_ref, mk_ref, mki_ref, md_ref,
                  mdi_ref, o_ref, z_ref, y_ref, u_ref, w_ref):
  c = HY_CB
  for half in range(2):
    z_ref[:, half * c:(half + 1) * c] = _conv3(pv_ref, half, cw_ref, 0)
  gates = (pa_ref, pb_ref)
  for order in range(HY_ORDER):
    _stage_k_forward(z_ref, mk_ref, u_ref)

    def spectral(f1, base, k, order=order):
      x = jnp.dot(md_ref[f1], _gather_f1(u_ref, base, k), preferred_element_type=F32)
      xr, xi = x[:DFT_F2], x[DFT_F2:]
      hr = h_ref[order, f1, pl.ds(0, DFT_F2), :]
      hi = h_ref[order, f1, pl.ds(DFT_F2, DFT_F2), :]
      hr = jnp.concatenate([hr, hr], axis=1)
      hi = jnp.concatenate([hi, hi], axis=1)
      y = jnp.concatenate([xr * hr - xi * hi, xr * hi + xi * hr], axis=0).astype(BF16)
      _scatter_f1(w_ref, base, k, jnp.dot(mdi_ref[f1], y, preferred_element_type=F32))

    _grouped_loop(DFT_F1, D_UNROLL, spectral)

    def emit(rows, tile):
      y_ref[rows, :] = tile

    _stage_k_inverse(w_ref, mki_ref, emit)
    for half in range(2):
      lanes = slice(half * c, (half + 1) * c)
      gate = _conv3(gates[order], half, cw_ref, 1 + order)
      z = gate * (y_ref[:, lanes] + skip_ref[order:order + 1, :] * z_ref[:, lanes])
      if order + 1 < HY_ORDER:
        z_ref[:, lanes] = z
      else:
        o_ref[pl.ds(half * SEQ, SEQ), :] = z.astype(BF16)


def _hyena(p, cw, skip, hspec, mk, mki, md, mdi, batch):
  c = HY_CB
  ncb = HY_WIDTH // c
  col = lambda k: pl.BlockSpec((2 * SEQ, c), lambda cb, bp: (bp, k * ncb + cb))
  return pl.pallas_call(
      _hyena_kernel,
      grid=(ncb, batch // 2),
      in_specs=[col(0), col(1), col(2),
                pl.BlockSpec((3, 4, c), lambda cb, bp: (0, 0, cb)),
                pl.BlockSpec((HY_ORDER, c), lambda cb, bp: (0, cb)),
                pl.BlockSpec((HY_ORDER, DFT_F1, 2 * DFT_F2, c), lambda cb, bp: (0, 0, 0, cb),
                             pipeline_mode=pl.Buffered(1)),
                _const_spec(mk.shape), _const_spec(mki.shape), _const_spec(md.shape),
                _const_spec(mdi.shape)],
      out_specs=pl.BlockSpec((2 * SEQ, c), lambda cb, bp: (bp, cb)),
      out_shape=jax.ShapeDtypeStruct((batch * SEQ, HY_WIDTH), BF16),
      scratch_shapes=[pltpu.VMEM((SEQ, 2 * c), F32), pltpu.VMEM((SEQ, 2 * c), F32),
                      pltpu.VMEM((DFT_GROUPS, UROWS, 2 * c), BF16),
                      pltpu.VMEM((DFT_GROUPS, UROWS, 2 * c), BF16)],
      compiler_params=pltpu.CompilerParams(
          dimension_semantics=("arbitrary", "arbitrary"), vmem_limit_bytes=VMEM_LIMIT_BYTES),
      name="hyena",
  )(p, p, p, cw, skip, hspec, mk, mki, md, mdi)


def _t5_bucket(rel):
  half = REL_BUCKETS // 2
  exact = half // 2
  ret = jnp.where(rel > 0, half, 0)
  n = jnp.abs(rel)
  nf = jnp.maximum(n, 1).astype(F32)
  large = exact + (jnp.log(nf / exact) / math.log(REL_MAX_DISTANCE / exact) * (half - exact)).astype(jnp.int32)
  large = jnp.minimum(large, half - 1)
  return ret + jnp.where(n < exact, n, large)


def _bucket_maps():
  qi = jnp.arange(ATT_BQ)[:, None]
  ki = jnp.arange(ATT_BK)[None, :]
  rel = ki - N_SIDE - qi
  maps = [jnp.where(jnp.abs(rel) <= N_SIDE, _t5_bucket(rel * d), -1) for d in DILATIONS]
  return jnp.stack(maps).astype(jnp.int32)


def _head_norm(x, gain, lane_lo):
  sq = x * x
  s_lo = jnp.sum(jnp.where(lane_lo, sq, 0.0), axis=-1, keepdims=True)
  s_all = jnp.sum(sq, axis=-1, keepdims=True)
  ms = jnp.where(lane_lo, s_lo, s_all - s_lo) * (1.0 / HEAD_DIM)
  return x * lax.rsqrt(ms + EPS) * gain


def _dil_attn_kernel(tab_ref, *refs):
  qkv_refs = refs[:9]
  bkt_ref, qg_ref, kg_ref, o_ref = refs[9:13]
  qs_ref, ks_ref, vs_ref, acc_ref, m_ref, l_ref, bias_ref = refs[13:]
  pair = pl.program_id(1)
  lane_lo = lax.broadcasted_iota(jnp.int32, (1, LANES), 1) < HEAD_DIM
  key_idx = lax.broadcasted_iota(jnp.int32, (1, ATT_BK), 1)

  for g, d in enumerate(DILATIONS):
    q_ref, k_ref, v_ref = qkv_refs[3 * g:3 * g + 3]
    m_sub = SEQ // d
    pad = N_SIDE * d
    bkt = bkt_ref[g]
    for h in range(2):
      col = g * HEADS_PER_GROUP + 2 * pair + h
      b = jnp.where(bkt < 0, NEG, 0.0)
      for t in range(REL_BUCKETS):
        b = b + jnp.where(bkt == t, tab_ref[t, col], 0.0)
      bias_ref[h] = b
    qs_ref[...] = _head_norm(q_ref[...].astype(F32), qg_ref[...], lane_lo) * (HEAD_DIM ** -0.5)
    zeros = jnp.zeros((pad, LANES), F32)
    ks_ref[pl.ds(0, pad), :] = zeros
    ks_ref[pl.ds(pad + SEQ, pad), :] = zeros
    vs_ref[pl.ds(0, pad), :] = zeros
    vs_ref[pl.ds(pad + SEQ, pad), :] = zeros
    ks_ref[pl.ds(pad, SEQ), :] = _head_norm(k_ref[...].astype(F32), kg_ref[...], lane_lo)
    vs_ref[pl.ds(pad, SEQ), :] = v_ref[...].astype(F32)

    def block(t, carry, d=d, m_sub=m_sub, g=g):
      r = t % d
      i = t // d
      start = r + d * ATT_BQ * i
      if d == 1:
        rows_q = pl.ds(start, ATT_BQ)
        rows_k = pl.ds(start, ATT_BK)
      else:
        rows_q = pl.ds(start, ATT_BQ, stride=d)
        rows_k = pl.ds(start, ATT_BK, stride=d)
      q = qs_ref[rows_q, :]
      k = ks_ref[rows_k, :].astype(BF16)
      v = vs_ref[rows_k, :].astype(BF16)
      kpos = ATT_BQ * i - N_SIDE + key_idx
      kvalid = (kpos >= 0) & (kpos < m_sub)
      outs, maxs, sums = [], [], []
      for h in range(2):
        qh = jnp.where(lane_lo if h == 0 else jnp.logical_not(lane_lo), q, 0.0).astype(BF16)
        s = lax.dot_general(qh, k, (((1,), (1,)), ((), ())), preferred_element_type=F32)
        s = jnp.where(kvalid, s + bias_ref[h], NEG)
        mx = jnp.max(s, axis=-1, keepdims=True)
        p = jnp.exp(s - mx)
        sums.append(jnp.sum(p, axis=-1, keepdims=True))
        maxs.append(mx)
        outs.append(jnp.dot(p.astype(BF16), v, preferred_element_type=F32))
      o = jnp.where(lane_lo, outs[0], outs[1])
      mx = jnp.where(lane_lo, maxs[0], maxs[1])
      sm = jnp.where(lane_lo, sums[0], sums[1])
      if g == 0:
        acc_ref[rows_q, :] = o
        m_ref[rows_q, :] = mx
        l_ref[rows_q, :] = sm
      else:
        m_old = m_ref[rows_q, :]
        m_new = jnp.maximum(m_old, mx)
        a_old = jnp.exp(m_old - m_new)
        a_new = jnp.exp(mx - m_new)
        acc_ref[rows_q, :] = acc_ref[rows_q, :] * a_old + o * a_new
        l_ref[rows_q, :] = l_ref[rows_q, :] * a_old + sm * a_new
        m_ref[rows_q, :] = m_new
      return carry

    lax.fori_loop(0, SEQ // ATT_BQ, block, 0, unroll=ATT_UNROLL)

  o_ref[...] = (acc_ref[...] / l_ref[...]).astype(BF16)


def _dil_attn(p, rel_bias, bkt, qg, kg, batch):
  base = 3 * HY_WIDTH // LANES
  per_kind = ATT_WIDTH // LANES
  per_group = ATT_OUT // LANES

  def col(kind, g):
    return pl.BlockSpec(
        (SEQ, LANES), lambda b, pr, tab: (b, base + kind * per_kind + g * per_group + pr))

  in_specs = [col(kind, g) for g in range(N_GROUPS) for kind in range(3)]
  in_specs += [pl.BlockSpec(bkt.shape, lambda b, pr, tab: (0, 0, 0)),
               pl.BlockSpec((1, LANES), lambda b, pr, tab: (0, 0)),
               pl.BlockSpec((1, LANES), lambda b, pr, tab: (0, 0))]
  grid_spec = pltpu.PrefetchScalarGridSpec(
      num_scalar_prefetch=1,
      grid=(batch, per_group),
      in_specs=in_specs,
      out_specs=pl.BlockSpec((SEQ, LANES), lambda b, pr, tab: (b, pr)),
      scratch_shapes=[pltpu.VMEM((SEQ, LANES), F32),
                      pltpu.VMEM((SEQ + 2 * ATT_PAD, LANES), F32),
                      pltpu.VMEM((SEQ + 2 * ATT_PAD, LANES), F32),
                      pltpu.VMEM((SEQ, LANES), F32), pltpu.VMEM((SEQ, LANES), F32),
                      pltpu.VMEM((SEQ, LANES), F32),
                      pltpu.VMEM((2, ATT_BQ, ATT_BK), F32)])
  return pl.pallas_call(
      _dil_attn_kernel,
      grid_spec=grid_spec,
      out_shape=jax.ShapeDtypeStruct((batch * SEQ, ATT_OUT), BF16),
      compiler_params=pltpu.CompilerParams(
          dimension_semantics=("arbitrary", "arbitrary"), vmem_limit_bytes=VMEM_LIMIT_BYTES),
      name="dil_attn",
  )(rel_bias, *([p] * 9), bkt, qg, kg)


def _filter_features():
  t = jnp.linspace(0.0, 1.0, SEQ, dtype=F32)[:, None]
  w = (2.0 * math.pi / SEQ) * jnp.arange(SEQ, dtype=F32)[:, None]
  f = jnp.linspace(1e-4, HY_POS_BANDS - 1, HY_POS_BANDS, dtype=F32)[None]
  z = jnp.concatenate([t, jnp.cos(f * w), -jnp.sin(f * w)], axis=-1)
  return jnp.pad(z, ((0, 0), (0, LANES - HY_EMB)))


def _decay_rates():
  max_decay = math.log(HY_DECAY_TARGET) / HY_FAST_DECAY_PCT
  min_decay = math.log(HY_DECAY_TARGET) / HY_SLOW_DECAY_PCT
  return jnp.abs(jnp.linspace(min_decay, max_decay, HY_WIDTH, dtype=F32))[None]


def kernel(x, rel_bias, ffn1_norm, ffn1_w_gate, ffn1_w_up, ffn1_w_down, mix_norm, w_in, w_gate, b_gate, hy_conv_w, hy_conv_b, hy_filt_w1, hy_filt_b1, hy_filt_w2, hy_filt_b2, hy_filt_w3, hy_skip, q_norm, k_norm, w_hy_proj, w_at_proj, w_out, ffn2_norm, ffn2_w_gate, ffn2_w_up, ffn2_w_down):
  batch, seq, d_model = x.shape
  assert (seq, d_model) == (SEQ, D_MODEL) and batch % 2 == 0
  depth = ffn1_norm.shape[0]
  mk, mki, md, mdi = _dft_constants()
  zfeat = _filter_features()
  deltas = _decay_rates()
  bkt = _bucket_maps()
  bf = lambda w: w.astype(BF16)
  row = lambda v: v.reshape(1, -1)

  xt = x.reshape(batch * seq, d_model)
  for l in range(depth):
    xt, p = _ffn_proj(xt, row(ffn1_norm[l]), bf(ffn1_w_gate[l]), bf(ffn1_w_up[l]),
                      bf(ffn1_w_down[l]), row(mix_norm[l]), bf(w_in[l]))
    w1 = jnp.pad(hy_filt_w1[l], ((0, LANES - HY_EMB), (0, 0)))
    hspec = _hy_filter(zfeat, w1, row(hy_filt_b1[l]), hy_filt_w2[l], row(hy_filt_b2[l]),
                       hy_filt_w3[l], deltas, mk, md)
    cw = jnp.concatenate([hy_conv_w[l], hy_conv_b[l][None]], axis=0)
    cw = cw.reshape(4, 3, HY_WIDTH).transpose(1, 0, 2)
    yhy = _hyena(p, cw, hy_skip[l], hspec, mk, mki, md, mdi, batch)
    yat = _dil_attn(p, rel_bias, bkt, row(jnp.tile(q_norm[l], 2)), row(jnp.tile(k_norm[l], 2)),
                    batch)
    xt = _merge_ffn(xt, yhy, yat, row(mix_norm[l]), bf(w_gate[l]), row(b_gate[l]),
                    bf(w_hy_proj[l]), bf(w_at_proj[l]), bf(w_out[l]), row(ffn2_norm[l]),
                    bf(ffn2_w_gate[l]), bf(ffn2_w_up[l]), bf(ffn2_w_down[l]))
  return xt.reshape(batch, seq, d_model)
```

```python
import functools
import math

import jax
import jax.numpy as jnp
import numpy as np
from jax import lax
from jax.experimental import pallas as pl
from jax.experimental.pallas import tpu as pltpu

F32 = jnp.float32
BF16 = jnp.bfloat16

D_MODEL = 1024
SEQ = 4096
HEAD_DIM = 64
HY_WIDTH = 512
HY_ORDER = 2
HY_FILT_HIDDEN = 64
HY_POS_BANDS = 16
HY_EMB = 1 + 2 * HY_POS_BANDS
HY_FAST_DECAY_PCT = 0.3
HY_SLOW_DECAY_PCT = 1.5
HY_DECAY_TARGET = 1e-2
WINDOWS = (128, 512, 2048)
DILATIONS = (1, 4, 16)
N_GROUPS = 3
HEADS_PER_GROUP = 4
ATT_WIDTH = N_GROUPS * HEADS_PER_GROUP * HEAD_DIM
ATT_OUT = HEADS_PER_GROUP * HEAD_DIM
REL_BUCKETS = 32
REL_MAX_DISTANCE = 1024
D_FF = (8 * D_MODEL // 3) // 128 * 128
IN_WIDTH = 3 * HY_WIDTH + 3 * ATT_WIDTH
EPS = 1e-6
NEG = -1e30

SUBLANES = 8
LANES = 128
BF16_ROWS = 16
VMEM_LIMIT_BYTES = 60 * 1024 * 1024

DFT_N = 2 * SEQ
DFT_N2 = 128
DFT_N1 = SEQ // DFT_N2
DFT_F1 = 64
DFT_F2 = SEQ // DFT_F1
DFT_GROUPS = DFT_N2 // SUBLANES
KROWS = DFT_N1 * SUBLANES
UROWS = 2 * DFT_F1 * SUBLANES

N_SIDE = 64
ATT_BQ = 128
ATT_BK = ATT_BQ + 2 * N_SIDE
ATT_PAD = N_SIDE * max(DILATIONS)

FFN_CHUNKS = ((0, 1280), (1280, D_FF))
TOKEN_TILE = 512
TOKEN_PARTS = 2
HY_CB = 128
K_UNROLL = 2
D_UNROLL = 16
ATT_UNROLL = 8


def _const_spec(shape):
  nd = len(shape)
  return pl.BlockSpec(shape, lambda *_: (0,) * nd, pipeline_mode=pl.Buffered(1))


def _row_parts(rows):
  part = rows // TOKEN_PARTS
  return [pl.ds(i * part, part) for i in range(TOKEN_PARTS)]


def _rms(x, g):
  return x * lax.rsqrt(jnp.mean(x * x, axis=-1, keepdims=True) + EPS) * g


def _swiglu_residual(x, g_ref, wg_ref, wu_ref, wd_ref):
  h = _rms(x, g_ref[...]).astype(BF16)
  acc = jnp.zeros_like(x)
  for c0, c1 in FFN_CHUNKS:
    a = jnp.dot(h, wg_ref[:, c0:c1], preferred_element_type=F32)
    u = jnp.dot(h, wu_ref[:, c0:c1], preferred_element_type=F32)
    s = (a * jax.nn.sigmoid(a) * u).astype(BF16)
    acc = acc + jnp.dot(s, wd_ref[c0:c1, :], preferred_element_type=F32)
  return x + 0.5 * acc


def _head_norm(x, gain, lane_lo):
  sq = x * x
  s_lo = jnp.sum(jnp.where(lane_lo, sq, 0.0), axis=-1, keepdims=True)
  s_all = jnp.sum(sq, axis=-1, keepdims=True)
  ms = jnp.where(lane_lo, s_lo, s_all - s_lo) * (1.0 / HEAD_DIM)
  return x * lax.rsqrt(ms + EPS) * gain


def _ffn_proj_kernel(x_ref, g1_ref, wg_ref, wu_ref, wd_ref, gm_ref, win_ref, qg_ref, kg_ref,
                     x1_ref, p_ref):
  lane_lo = lax.broadcasted_iota(jnp.int32, (1, LANES), 1) < HEAD_DIM
  q0 = 3 * HY_WIDTH
  head_gain = {q0: qg_ref[...] * (HEAD_DIM ** -0.5), q0 + ATT_WIDTH: kg_ref[...]}
  for rows in _row_parts(x_ref.shape[0]):
    x1 = _swiglu_residual(x_ref[rows, :], g1_ref, wg_ref, wu_ref, wd_ref)
    x1_ref[rows, :] = x1
    h = _rms(x1, gm_ref[...]).astype(BF16)
    for c0 in range(0, IN_WIDTH, ATT_WIDTH):
      y = jnp.dot(h, win_ref[:, c0:c0 + ATT_WIDTH], preferred_element_type=F32)
      if c0 in head_gain:
        y = jnp.concatenate(
            [_head_norm(y[:, s0:s0 + LANES], head_gain[c0], lane_lo)
             for s0 in range(0, ATT_WIDTH, LANES)], axis=1)
      p_ref[rows, c0:c0 + ATT_WIDTH] = y.astype(BF16)


def _ffn_proj(x, g1, wg, wu, wd, gm, win, qg, kg):
  t = x.shape[0]
  tm = TOKEN_TILE
  tok = lambda w: pl.BlockSpec((tm, w), lambda i: (i, 0))
  return pl.pallas_call(
      _ffn_proj_kernel,
      grid=(t // tm,),
      in_specs=[tok(D_MODEL), _const_spec((1, D_MODEL)), _const_spec((D_MODEL, D_FF)),
                _const_spec((D_MODEL, D_FF)), _const_spec((D_FF, D_MODEL)),
                _const_spec((1, D_MODEL)), _const_spec((D_MODEL, IN_WIDTH)),
                _const_spec((1, LANES)), _const_spec((1, LANES))],
      out_specs=[tok(D_MODEL), tok(IN_WIDTH)],
      out_shape=[jax.ShapeDtypeStruct((t, D_MODEL), F32), jax.ShapeDtypeStruct((t, IN_WIDTH), BF16)],
      compiler_params=pltpu.CompilerParams(
          dimension_semantics=("arbitrary",), vmem_limit_bytes=VMEM_LIMIT_BYTES),
      name="ffn_proj",
  )(x, g1, wg, wu, wd, gm, win, qg, kg)


def _merge_ffn_kernel(x_ref, yhy_ref, yat_ref, gm_ref, wgate_ref, bgate_ref, why_ref, wat_ref,
                      wout_ref, g2_ref, wg_ref, wu_ref, wd_ref, o_ref):
  for rows in _row_parts(x_ref.shape[0]):
    x = x_ref[rows, :]
    h = _rms(x, gm_ref[...]).astype(BF16)
    gates = jax.nn.sigmoid(
        jnp.dot(h, wgate_ref[...], preferred_element_type=F32) + bgate_ref[...])
    a = jnp.dot(yhy_ref[rows, :], why_ref[...], preferred_element_type=F32)
    b = jnp.dot(yat_ref[rows, :], wat_ref[...], preferred_element_type=F32)
    y = gates[:, :D_MODEL] * a + gates[:, D_MODEL:] * b
    x2 = x + jnp.dot(y.astype(BF16), wout_ref[...], preferred_element_type=F32)
    o_ref[rows, :] = _swiglu_residual(x2, g2_ref, wg_ref, wu_ref, wd_ref)


def _merge_ffn(x, yhy, yat, gm, wgate, bgate, why, wat, wout, g2, wg, wu, wd):
  t = x.shape[0]
  tm = TOKEN_TILE
  tok = lambda w: pl.BlockSpec((tm, w), lambda i: (i, 0))
  return pl.pallas_call(
      _merge_ffn_kernel,
      grid=(t // tm,),
      in_specs=[tok(D_MODEL), tok(HY_WIDTH), tok(ATT_OUT), _const_spec((1, D_MODEL)),
                _const_spec((D_MODEL, 2 * D_MODEL)), _const_spec((1, 2 * D_MODEL)),
                _const_spec((HY_WIDTH, D_MODEL)), _const_spec((ATT_OUT, D_MODEL)),
                _const_spec((D_MODEL, D_MODEL)), _const_spec((1, D_MODEL)),
                _const_spec((D_MODEL, D_FF)), _const_spec((D_MODEL, D_FF)),
                _const_spec((D_FF, D_MODEL))],
      out_specs=tok(D_MODEL),
      out_shape=jax.ShapeDtypeStruct((t, D_MODEL), F32),
      compiler_params=pltpu.CompilerParams(
          dimension_semantics=("arbitrary",), vmem_limit_bytes=VMEM_LIMIT_BYTES),
      name="merge_ffn",
  )(x, yhy, yat, gm, wgate, bgate, why, wat, wout, g2, wg, wu, wd)


@functools.lru_cache(maxsize=None)
def _dft_constants_host():
  f1 = np.arange(DFT_F1)
  n1 = np.arange(DFT_N1)
  f2 = np.arange(DFT_F2)
  n2 = np.arange(DFT_N2)
  eye = np.eye(SUBLANES)
  a_ang = 2.0 * np.pi * (((2 * f1 + 1)[:, None] * n1[None, :]) % DFT_N2) / DFT_N2
  ar, ai = np.cos(a_ang), -np.sin(a_ang)
  mk = np.stack([np.einsum("fn,st->fsnt", ar, eye), np.einsum("fn,st->fsnt", ai, eye)], axis=1)
  mk = mk.reshape(UROWS, KROWS)
  scale = 2.0 / DFT_N
  mki = np.stack([np.einsum("fn,st->ntfs", ar * scale, eye),
                  np.einsum("fn,st->ntfs", ai * scale, eye)], axis=3)
  mki = mki.reshape(KROWS, UROWS)
  ph = (n2[None, None, :] * (DFT_N2 * f2[None, :, None] + 2 * f1[:, None, None] + 1)) % (2 * DFT_N)
  d_ang = 2.0 * np.pi * ph / (2 * DFT_N)
  dr = np.cos(d_ang).reshape(DFT_F1, DFT_F2, DFT_GROUPS, SUBLANES)
  di = -np.sin(d_ang).reshape(DFT_F1, DFT_F2, DFT_GROUPS, SUBLANES)
  top = np.stack([dr, -di], axis=3)
  bot = np.stack([di, dr], axis=3)
  md = np.stack([top, bot], axis=1).reshape(DFT_F1, 2 * DFT_F2, 2 * DFT_N2)
  drt = dr.transpose(0, 2, 3, 1)
  dit = di.transpose(0, 2, 3, 1)
  re_rows = np.stack([drt, dit], axis=3)
  im_rows = np.stack([-dit, drt], axis=3)
  mdi = np.stack([re_rows, im_rows], axis=2).reshape(DFT_F1, 2 * DFT_N2, 2 * DFT_F2)
  return tuple(np.asarray(m, np.float32) for m in (mk, mki, md, mdi))


def _dft_constants():
  return tuple(jnp.asarray(m, dtype=BF16) for m in _dft_constants_host())


def _grouped_loop(count, group, body):
  def outer(i, carry):
    base = i * group
    for k in range(group):
      body(base + k, base, k)
    return carry

  lax.fori_loop(0, count // group, outer, 0)


def _tile_rows(base, scale, offset, size):
  return pl.ds(pl.multiple_of(base * scale, size) + offset, size)


def _stage_k_forward(z_ref, mk_ref, u_ref):
  def body(j, base, k):
    tiles = [z_ref[_tile_rows(base, SUBLANES, DFT_N2 * n1 + SUBLANES * k, SUBLANES), :]
             for n1 in range(DFT_N1)]
    rhs = jnp.concatenate(tiles, axis=0).astype(BF16)
    u_ref[j] = jnp.dot(mk_ref[...], rhs, preferred_element_type=F32).astype(BF16)

  _grouped_loop(DFT_GROUPS, K_UNROLL, body)


def _gather_f1(u_ref, base, k):
  rows = _tile_rows(base, BF16_ROWS, BF16_ROWS * k, BF16_ROWS)
  return jnp.concatenate([u_ref[j, rows, :] for j in range(DFT_GROUPS)], axis=0)


def _scatter_f1(u_ref, base, k, w):
  rows = _tile_rows(base, BF16_ROWS, BF16_ROWS * k, BF16_ROWS)
  wb = w.astype(BF16)
  for j in range(DFT_GROUPS):
    u_ref[j, rows, :] = wb[BF16_ROWS * j:BF16_ROWS * (j + 1), :]


def _stage_k_inverse(u_ref, mki_ref, emit):
  def body(j, base, k):
    y = jnp.dot(mki_ref[...], u_ref[j], preferred_element_type=F32)
    for n1 in range(DFT_N1):
      rows = _tile_rows(base, SUBLANES, DFT_N2 * n1 + SUBLANES * k, SUBLANES)
      emit(rows, y[SUBLANES * n1:SUBLANES * (n1 + 1), :])

  _grouped_loop(DFT_GROUPS, K_UNROLL, body)


def _hy_filter_kernel(z_ref, w1_ref, b1_ref, w2_ref, b2_ref, w3f_ref, w3b_ref, dl_ref,
                      mk_ref, md_ref, h_ref, hid_ref, k_ref, u_ref):
  hp = lax.Precision.HIGHEST

  @pl.when((pl.program_id(0) == 0) & (pl.program_id(1) == 0))
  def _():
    hid = jnp.sin(
        jnp.dot(z_ref[...], w1_ref[...], precision=hp, preferred_element_type=F32) + b1_ref[...])
    hid_ref[...] = jnp.sin(
        jnp.dot(hid, w2_ref[...], precision=hp, preferred_element_type=F32) + b2_ref[...])

  hid = hid_ref[...]
  decay = jnp.exp(-z_ref[:, 0:1] * dl_ref[...])
  kf = jnp.dot(hid, w3f_ref[...], precision=hp, preferred_element_type=F32) * decay
  kb = jnp.dot(hid, w3b_ref[...], precision=hp, preferred_element_type=F32) * decay
  row = lax.broadcasted_iota(jnp.int32, kb.shape, 0)
  kb = jnp.where(row == 0, 0.0, kb)
  rs = lax.rsqrt(jnp.sum(kf * kf + kb * kb, axis=0, keepdims=True) + EPS)
  c = kf.shape[1]
  k_ref[:, :c] = kf
  k_ref[:, c:] = kb
  _stage_k_forward(k_ref, mk_ref, u_ref)

  def body(f1, base, k):
    x = jnp.dot(md_ref[f1], _gather_f1(u_ref, base, k), preferred_element_type=F32)
    h_ref[0, f1, pl.ds(0, DFT_F2), :] = (x[:DFT_F2, :c] + x[:DFT_F2, c:]) * rs
    h_ref[0, f1, pl.ds(DFT_F2, DFT_F2), :] = (x[DFT_F2:, :c] - x[DFT_F2:, c:]) * rs

  _grouped_loop(DFT_F1, D_UNROLL, body)


def _hy_filter(zfeat, w1, b1, w2, b2, w3, deltas, mk, md):
  c = HY_CB
  ncb = HY_WIDTH // c
  return pl.pallas_call(
      _hy_filter_kernel,
      grid=(HY_ORDER, ncb),
      in_specs=[_const_spec(zfeat.shape), _const_spec(w1.shape), _const_spec(b1.shape),
                _const_spec(w2.shape), _const_spec(b2.shape),
                pl.BlockSpec((HY_FILT_HIDDEN, c), lambda o, cb: (0, o * 2 * ncb + cb)),
                pl.BlockSpec((HY_FILT_HIDDEN, c), lambda o, cb: (0, o * 2 * ncb + ncb + cb)),
                pl.BlockSpec((1, c), lambda o, cb: (0, cb)),
                _const_spec(mk.shape), _const_spec(md.shape)],
      out_specs=pl.BlockSpec((1, DFT_F1, 2 * DFT_F2, c), lambda o, cb: (o, 0, 0, cb)),
      out_shape=jax.ShapeDtypeStruct((HY_ORDER, DFT_F1, 2 * DFT_F2, HY_WIDTH), F32),
      scratch_shapes=[pltpu.VMEM((SEQ, HY_FILT_HIDDEN), F32),
                      pltpu.VMEM((SEQ, 2 * c), F32),
                      pltpu.VMEM((DFT_GROUPS, UROWS, 2 * c), BF16)],
      compiler_params=pltpu.CompilerParams(
          dimension_semantics=("arbitrary", "arbitrary"), vmem_limit_bytes=VMEM_LIMIT_BYTES),
      name="hy_filter",
  )(zfeat, w1, b1, w2, b2, w3, w3, deltas, mk, md)


def _conv3(p_ref, half, cw_ref, k):
  u = p_ref[pl.ds(half * SEQ, SEQ), :].astype(F32)
  row = lax.broadcasted_iota(jnp.int32, u.shape, 0)
  prev = jnp.where(row == 0, 0.0, pltpu.roll(u, 1, 0))
  nxt = jnp.where(row == SEQ - 1, 0.0, pltpu.roll(u, SEQ - 1, 0))
  w = cw_ref[k]
  return prev * w[0:1] + u * w[1:2] + nxt * w[2:3] + w[3:4]


def _hyena_kernel(pv_ref, pa_ref, pb_ref, cw_ref, skip_ref, h_ref, mk_ref, mki_ref, md_ref,
                  mdi_ref, o_ref, z_ref, y_ref, u_ref, w_ref):
  c = HY_CB
  for half in range(2):
    z_ref[:, half * c:(half + 1) * c] = _conv3(pv_ref, half, cw_ref, 0)
  gates = (pa_ref, pb_ref)
  for order in range(HY_ORDER):
    _stage_k_forward(z_ref, mk_ref, u_ref)

    def spectral(f1, base, k, order=order):
      x = jnp.dot(md_ref[f1], _gather_f1(u_ref, base, k), preferred_element_type=F32)
      xr, xi = x[:DFT_F2], x[DFT_F2:]
      hr = h_ref[order, f1, pl.ds(0, DFT_F2), :]
      hi = h_ref[order, f1, pl.ds(DFT_F2, DFT_F2), :]
      hr = jnp.concatenate([hr, hr], axis=1)
      hi = jnp.concatenate([hi, hi], axis=1)
      y = jnp.concatenate([xr * hr - xi * hi, xr * hi + xi * hr], axis=0).astype(BF16)
      _scatter_f1(w_ref, base, k, jnp.dot(mdi_ref[f1], y, preferred_element_type=F32))

    _grouped_loop(DFT_F1, D_UNROLL, spectral)

    def emit(rows, tile):
      y_ref[rows, :] = tile

    _stage_k_inverse(w_ref, mki_ref, emit)
    for half in range(2):
      lanes = slice(half * c, (half + 1) * c)
      gate = _conv3(gates[order], half, cw_ref, 1 + order)
      z = gate * (y_ref[:, lanes] + skip_ref[order:order + 1, :] * z_ref[:, lanes])
      if order + 1 < HY_ORDER:
        z_ref[:, lanes] = z
      else:
        o_ref[pl.ds(half * SEQ, SEQ), :] = z.astype(BF16)


def _hyena(p, cw, skip, hspec, mk, mki, md, mdi, batch):
  c = HY_CB
  ncb = HY_WIDTH // c
  col = lambda k: pl.BlockSpec((2 * SEQ, c), lambda cb, bp: (bp, k * ncb + cb))
  return pl.pallas_call(
      _hyena_kernel,
      grid=(ncb, batch // 2),
      in_specs=[col(0), col(1), col(2),
                pl.BlockSpec((3, 4, c), lambda cb, bp: (0, 0, cb)),
                pl.BlockSpec((HY_ORDER, c), lambda cb, bp: (0, cb)),
                pl.BlockSpec((HY_ORDER, DFT_F1, 2 * DFT_F2, c), lambda cb, bp: (0, 0, 0, cb),
                             pipeline_mode=pl.Buffered(1)),
                _const_spec(mk.shape), _const_spec(mki.shape), _const_spec(md.shape),
                _const_spec(mdi.shape)],
      out_specs=pl.BlockSpec((2 * SEQ, c), lambda cb, bp: (bp, cb)),
      out_shape=jax.ShapeDtypeStruct((batch * SEQ, HY_WIDTH), BF16),
      scratch_shapes=[pltpu.VMEM((SEQ, 2 * c), F32), pltpu.VMEM((SEQ, 2 * c), F32),
                      pltpu.VMEM((DFT_GROUPS, UROWS, 2 * c), BF16),
                      pltpu.VMEM((DFT_GROUPS, UROWS, 2 * c), BF16)],
      compiler_params=pltpu.CompilerParams(
          dimension_semantics=("arbitrary", "arbitrary"), vmem_limit_bytes=VMEM_LIMIT_BYTES),
      name="hyena",
  )(p, p, p, cw, skip, hspec, mk, mki, md, mdi)


def _t5_bucket(rel):
  half = REL_BUCKETS // 2
  exact = half // 2
  ret = jnp.where(rel > 0, half, 0)
  n = jnp.abs(rel)
  nf = jnp.maximum(n, 1).astype(F32)
  large = exact + (jnp.log(nf / exact) / math.log(REL_MAX_DISTANCE / exact) * (half - exact)).astype(jnp.int32)
  large = jnp.minimum(large, half - 1)
  return ret + jnp.where(n < exact, n, large)


def _bucket_maps():
  qi = jnp.arange(ATT_BQ)[:, None]
  ki = jnp.arange(ATT_BK)[None, :]
  rel = ki - N_SIDE - qi
  maps = [jnp.where(jnp.abs(rel) <= N_SIDE, _t5_bucket(rel * d), -1) for d in DILATIONS]
  return jnp.stack(maps).astype(jnp.int32)


def _dil_attn_kernel(tab_ref, *refs):
  qkv_refs = refs[:9]
  bkt_ref, o_ref = refs[9:11]
  qs_ref, ks_ref, vs_ref, acc_ref, m_ref, l_ref, bias_ref = refs[11:]
  pair = pl.program_id(1)
  lane_lo = lax.broadcasted_iota(jnp.int32, (1, LANES), 1) < HEAD_DIM
  key_idx = lax.broadcasted_iota(jnp.int32, (1, ATT_BK), 1)

  @pl.when((pl.program_id(0) == 0) & (pair == 0))
  def _():
    for g in range(N_GROUPS):
      bkt = bkt_ref[g]
      outside = jnp.where(bkt < 0, NEG, 0.0)
      for c in range(HEADS_PER_GROUP):
        b = outside
        for t in range(REL_BUCKETS):
          b = b + jnp.where(bkt == t, tab_ref[t, g * HEADS_PER_GROUP + c], 0.0)
        bias_ref[g * HEADS_PER_GROUP + c] = b

  for g, d in enumerate(DILATIONS):
    q_ref, k_ref, v_ref = qkv_refs[3 * g:3 * g + 3]
    m_sub = SEQ // d
    pad = N_SIDE * d
    qs_ref[...] = q_ref[...].astype(F32)
    zeros = jnp.zeros((pad, LANES), F32)
    ks_ref[pl.ds(0, pad), :] = zeros
    ks_ref[pl.ds(pad + SEQ, pad), :] = zeros
    vs_ref[pl.ds(0, pad), :] = zeros
    vs_ref[pl.ds(pad + SEQ, pad), :] = zeros
    ks_ref[pl.ds(pad, SEQ), :] = k_ref[...].astype(F32)
    vs_ref[pl.ds(pad, SEQ), :] = v_ref[...].astype(F32)

    def block(t, carry, d=d, m_sub=m_sub, g=g):
      r = t % d
      i = t // d
      start = r + d * ATT_BQ * i
      if d == 1:
        rows_q = pl.ds(start, ATT_BQ)
        rows_k = pl.ds(start, ATT_BK)
      else:
        rows_q = pl.ds(start, ATT_BQ, stride=d)
        rows_k = pl.ds(start, ATT_BK, stride=d)
      q = qs_ref[rows_q, :]
      k = ks_ref[rows_k, :].astype(BF16)
      v = vs_ref[rows_k, :].astype(BF16)
      kpos = ATT_BQ * i - N_SIDE + key_idx
      kvalid = (kpos >= 0) & (kpos < m_sub)
      outs, maxs, sums = [], [], []
      for h in range(2):
        qh = jnp.where(lane_lo if h == 0 else jnp.logical_not(lane_lo), q, 0.0).astype(BF16)
        s = lax.dot_general(qh, k, (((1,), (1,)), ((), ())), preferred_element_type=F32)
        s = jnp.where(kvalid, s + bias_ref[g * HEADS_PER_GROUP + 2 * pair + h], NEG)
        mx = jnp.max(s, axis=-1, keepdims=True)
        p = jnp.exp(s - mx)
        sums.append(jnp.sum(p, axis=-1, keepdims=True))
        maxs.append(mx)
        outs.append(jnp.dot(p.astype(BF16), v, preferred_element_type=F32))
      o = jnp.where(lane_lo, outs[0], outs[1])
      mx = jnp.where(lane_lo, maxs[0], maxs[1])
      sm = jnp.where(lane_lo, sums[0], sums[1])
      if g == 0:
        acc_ref[rows_q, :] = o
        m_ref[rows_q, :] = mx
        l_ref[rows_q, :] = sm
      else:
        m_old = m_ref[rows_q, :]
        m_new = jnp.maximum(m_old, mx)
        a_old = jnp.exp(m_old - m_new)
        a_new = jnp.exp(mx - m_new)
        acc_ref[rows_q, :] = acc_ref[rows_q, :] * a_old + o * a_new
        l_ref[rows_q, :] = l_ref[rows_q, :] * a_old + sm * a_new
        m_ref[rows_q, :] = m_new
      return carry

    lax.fori_loop(0, SEQ // ATT_BQ, block, 0, unroll=ATT_UNROLL)

  o_ref[...] = (acc_ref[...] / l_ref[...]).astype(BF16)


def _dil_attn(p, rel_bias, bkt, batch):
  base = 3 * HY_WIDTH // LANES
  per_kind = ATT_WIDTH // LANES
  per_group = ATT_OUT // LANES

  def col(kind, g):
    return pl.BlockSpec(
        (SEQ, LANES), lambda b, pr, tab: (b, base + kind * per_kind + g * per_group + pr))

  in_specs = [col(kind, g) for g in range(N_GROUPS) for kind in range(3)]
  in_specs += [pl.BlockSpec(bkt.shape, lambda b, pr, tab: (0, 0, 0))]
  grid_spec = pltpu.PrefetchScalarGridSpec(
      num_scalar_prefetch=1,
      grid=(batch, per_group),
      in_specs=in_specs,
      out_specs=pl.BlockSpec((SEQ, LANES), lambda b, pr, tab: (b, pr)),
      scratch_shapes=[pltpu.VMEM((SEQ, LANES), F32),
                      pltpu.VMEM((SEQ + 2 * ATT_PAD, LANES), F32),
                      pltpu.VMEM((SEQ + 2 * ATT_PAD, LANES), F32),
                      pltpu.VMEM((SEQ, LANES), F32), pltpu.VMEM((SEQ, LANES), F32),
                      pltpu.VMEM((SEQ, LANES), F32),
                      pltpu.VMEM((N_GROUPS * HEADS_PER_GROUP, ATT_BQ, ATT_BK), F32)])
  return pl.pallas_call(
      _dil_attn_kernel,
      grid_spec=grid_spec,
      out_shape=jax.ShapeDtypeStruct((batch * SEQ, ATT_OUT), BF16),
      compiler_params=pltpu.CompilerParams(
          dimension_semantics=("arbitrary", "arbitrary"), vmem_limit_bytes=VMEM_LIMIT_BYTES),
      name="dil_attn",
  )(rel_bias, *([p] * 9), bkt)


def _filter_features():
  t = jnp.linspace(0.0, 1.0, SEQ, dtype=F32)[:, None]
  w = (2.0 * math.pi / SEQ) * jnp.arange(SEQ, dtype=F32)[:, None]
  f = jnp.linspace(1e-4, HY_POS_BANDS - 1, HY_POS_BANDS, dtype=F32)[None]
  z = jnp.concatenate([t, jnp.cos(f * w), -jnp.sin(f * w)], axis=-1)
  return jnp.pad(z, ((0, 0), (0, LANES - HY_EMB)))


def _decay_rates():
  max_decay = math.log(HY_DECAY_TARGET) / HY_FAST_DECAY_PCT
  min_decay = math.log(HY_DECAY_TARGET) / HY_SLOW_DECAY_PCT
  return jnp.abs(jnp.linspace(min_decay, max_decay, HY_WIDTH, dtype=F32))[None]


def kernel(x, rel_bias, ffn1_norm, ffn1_w_gate, ffn1_w_up, ffn1_w_down, mix_norm, w_in, w_gate, b_gate, hy_conv_w, hy_conv_b, hy_filt_w1, hy_filt_b1, hy_filt_w2, hy_filt_b2, hy_filt_w3, hy_skip, q_norm, k_norm, w_hy_proj, w_at_proj, w_out, ffn2_norm, ffn2_w_gate, ffn2_w_up, ffn2_w_down):
  batch, seq, d_model = x.shape
  assert (seq, d_model) == (SEQ, D_MODEL) and batch % 2 == 0
  depth = ffn1_norm.shape[0]
  mk, mki, md, mdi = _dft_constants()
  zfeat = _filter_features()
  deltas = _decay_rates()
  bkt = _bucket_maps()
  bf = lambda w: w.astype(BF16)
  row = lambda v: v.reshape(1, -1)

  xt = x.reshape(batch * seq, d_model)
  for l in range(depth):
    xt, p = _ffn_proj(xt, row(ffn1_norm[l]), bf(ffn1_w_gate[l]), bf(ffn1_w_up[l]),
                      bf(ffn1_w_down[l]), row(mix_norm[l]), bf(w_in[l]),
                      row(jnp.tile(q_norm[l], 2)), row(jnp.tile(k_norm[l], 2)))
    w1 = jnp.pad(hy_filt_w1[l], ((0, LANES - HY_EMB), (0, 0)))
    hspec = _hy_filter(zfeat, w1, row(hy_filt_b1[l]), hy_filt_w2[l], row(hy_filt_b2[l]),
                       hy_filt_w3[l], deltas, mk, md)
    cw = jnp.concatenate([hy_conv_w[l], hy_conv_b[l][None]], axis=0)
    cw = cw.reshape(4, 3, HY_WIDTH).transpose(1, 0, 2)
    yhy = _hyena(p, cw, hy_skip[l], hspec, mk, mki, md, mdi, batch)
    yat = _dil_attn(p, rel_bias, bkt, batch)
    xt = _merge_ffn(xt, yhy, yat, row(mix_norm[l]), bf(w_gate[l]), row(b_gate[l]),
                    bf(w_hy_proj[l]), bf(w_at_proj[l]), bf(w_out[l]), row(ffn2_norm[l]),
                    bf(ffn2_w_gate[l]), bf(ffn2_w_up[l]), bf(ffn2_w_down[l]))
  return xt.reshape(batch, seq, d_model)
```

```python
import functools
import math

import jax
import jax.numpy as jnp
import numpy as np
from jax import lax
from jax.experimental import pallas as pl
from jax.experimental.pallas import tpu as pltpu

F32 = jnp.float32
BF16 = jnp.bfloat16

D_MODEL = 1024
SEQ = 4096
HEAD_DIM = 64
HY_WIDTH = 512
HY_ORDER = 2
HY_FILT_HIDDEN = 64
HY_POS_BANDS = 16
HY_EMB = 1 + 2 * HY_POS_BANDS
HY_FAST_DECAY_PCT = 0.3
HY_SLOW_DECAY_PCT = 1.5
HY_DECAY_TARGET = 1e-2
WINDOWS = (128, 512, 2048)
DILATIONS = (1, 4, 16)
N_GROUPS = 3
HEADS_PER_GROUP = 4
ATT_WIDTH = N_GROUPS * HEADS_PER_GROUP * HEAD_DIM
ATT_OUT = HEADS_PER_GROUP * HEAD_DIM
REL_BUCKETS = 32
REL_MAX_DISTANCE = 1024
D_FF = (8 * D_MODEL // 3) // 128 * 128
IN_WIDTH = 3 * HY_WIDTH + 3 * ATT_WIDTH
EPS = 1e-6
NEG = -1e30

SUBLANES = 8
LANES = 128
BF16_ROWS = 16
VMEM_LIMIT_BYTES = 60 * 1024 * 1024

DFT_N = 2 * SEQ
DFT_N2 = 128
DFT_N1 = SEQ // DFT_N2
DFT_F1 = 64
DFT_F2 = SEQ // DFT_F1
DFT_GROUPS = DFT_N2 // SUBLANES
KROWS = DFT_N1 * SUBLANES
UROWS = 2 * DFT_F1 * SUBLANES

N_SIDE = 64
ATT_BQ = 128
ATT_BK = ATT_BQ + 2 * N_SIDE
ATT_PAD = N_SIDE * max(DILATIONS)

FFN_CHUNKS = ((0, 1280), (1280, D_FF))
TOKEN_TILE = 512
TOKEN_PARTS = 2
HY_CB = 128
K_UNROLL = 8
D_UNROLL = 16
ATT_UNROLL = 8


def _const_spec(shape):
  nd = len(shape)
  return pl.BlockSpec(shape, lambda *_: (0,) * nd, pipeline_mode=pl.Buffered(1))


def _row_parts(rows):
  part = rows // TOKEN_PARTS
  return [pl.ds(i * part, part) for i in range(TOKEN_PARTS)]


def _rms(x, g):
  return x * lax.rsqrt(jnp.mean(x * x, axis=-1, keepdims=True) + EPS) * g


def _swiglu_residual(x, g_ref, wg_ref, wu_ref, wd_ref):
  h = _rms(x, g_ref[...]).astype(BF16)
  acc = jnp.zeros_like(x)
  for c0, c1 in FFN_CHUNKS:
    a = jnp.dot(h, wg_ref[:, c0:c1], preferred_element_type=F32)
    u = jnp.dot(h, wu_ref[:, c0:c1], preferred_element_type=F32)
    s = (a * jax.nn.sigmoid(a) * u).astype(BF16)
    acc = acc + jnp.dot(s, wd_ref[c0:c1, :], preferred_element_type=F32)
  return x + 0.5 * acc


def _head_norm(x, gain, lane_lo):
  sq = x * x
  s_lo = jnp.sum(jnp.where(lane_lo, sq, 0.0), axis=-1, keepdims=True)
  s_all = jnp.sum(sq, axis=-1, keepdims=True)
  ms = jnp.where(lane_lo, s_lo, s_all - s_lo) * (1.0 / HEAD_DIM)
  return x * lax.rsqrt(ms + EPS) * gain


def _ffn_proj_kernel(x_ref, g1_ref, wg_ref, wu_ref, wd_ref, gm_ref, win_ref, qg_ref, kg_ref,
                     x1_ref, p_ref):
  lane_lo = lax.broadcasted_iota(jnp.int32, (1, LANES), 1) < HEAD_DIM
  q0 = 3 * HY_WIDTH
  head_gain = {q0: qg_ref[...] * (HEAD_DIM ** -0.5), q0 + ATT_WIDTH: kg_ref[...]}
  for rows in _row_parts(x_ref.shape[0]):
    x1 = _swiglu_residual(x_ref[rows, :], g1_ref, wg_ref, wu_ref, wd_ref)
    x1_ref[rows, :] = x1
    h = _rms(x1, gm_ref[...]).astype(BF16)
    for c0 in range(0, IN_WIDTH, ATT_WIDTH):
      y = jnp.dot(h, win_ref[:, c0:c0 + ATT_WIDTH], preferred_element_type=F32)
      if c0 in head_gain:
        y = jnp.concatenate(
            [_head_norm(y[:, s0:s0 + LANES], head_gain[c0], lane_lo)
             for s0 in range(0, ATT_WIDTH, LANES)], axis=1)
      p_ref[rows, c0:c0 + ATT_WIDTH] = y.astype(BF16)


def _ffn_proj(x, g1, wg, wu, wd, gm, win, qg, kg):
  t = x.shape[0]
  tm = TOKEN_TILE
  tok = lambda w: pl.BlockSpec((tm, w), lambda i: (i, 0))
  return pl.pallas_call(
      _ffn_proj_kernel,
      grid=(t // tm,),
      in_specs=[tok(D_MODEL), _const_spec((1, D_MODEL)), _const_spec((D_MODEL, D_FF)),
                _const_spec((D_MODEL, D_FF)), _const_spec((D_FF, D_MODEL)),
                _const_spec((1, D_MODEL)), _const_spec((D_MODEL, IN_WIDTH)),
                _const_spec((1, LANES)), _const_spec((1, LANES))],
      out_specs=[tok(D_MODEL), tok(IN_WIDTH)],
      out_shape=[jax.ShapeDtypeStruct((t, D_MODEL), F32), jax.ShapeDtypeStruct((t, IN_WIDTH), BF16)],
      compiler_params=pltpu.CompilerParams(
          dimension_semantics=("arbitrary",), vmem_limit_bytes=VMEM_LIMIT_BYTES),
      name="ffn_proj",
  )(x, g1, wg, wu, wd, gm, win, qg, kg)


def _merge_ffn_kernel(x_ref, yhy_ref, yat_ref, gm_ref, wgate_ref, bgate_ref, why_ref, wat_ref,
                      wout_ref, g2_ref, wg_ref, wu_ref, wd_ref, o_ref):
  for rows in _row_parts(x_ref.shape[0]):
    x = x_ref[rows, :]
    h = _rms(x, gm_ref[...]).astype(BF16)
    gates = jax.nn.sigmoid(
        jnp.dot(h, wgate_ref[...], preferred_element_type=F32) + bgate_ref[...])
    a = jnp.dot(yhy_ref[rows, :], why_ref[...], preferred_element_type=F32)
    b = jnp.dot(yat_ref[rows, :], wat_ref[...], preferred_element_type=F32)
    y = gates[:, :D_MODEL] * a + gates[:, D_MODEL:] * b
    x2 = x + jnp.dot(y.astype(BF16), wout_ref[...], preferred_element_type=F32)
    o_ref[rows, :] = _swiglu_residual(x2, g2_ref, wg_ref, wu_ref, wd_ref)


def _merge_ffn(x, yhy, yat, gm, wgate, bgate, why, wat, wout, g2, wg, wu, wd):
  t = x.shape[0]
  tm = TOKEN_TILE
  tok = lambda w: pl.BlockSpec((tm, w), lambda i: (i, 0))
  return pl.pallas_call(
      _merge_ffn_kernel,
      grid=(t // tm,),
      in_specs=[tok(D_MODEL), tok(HY_WIDTH), tok(ATT_OUT), _const_spec((1, D_MODEL)),
                _const_spec((D_MODEL, 2 * D_MODEL)), _const_spec((1, 2 * D_MODEL)),
                _const_spec((HY_WIDTH, D_MODEL)), _const_spec((ATT_OUT, D_MODEL)),
                _const_spec((D_MODEL, D_MODEL)), _const_spec((1, D_MODEL)),
                _const_spec((D_MODEL, D_FF)), _const_spec((D_MODEL, D_FF)),
                _const_spec((D_FF, D_MODEL))],
      out_specs=tok(D_MODEL),
      out_shape=jax.ShapeDtypeStruct((t, D_MODEL), F32),
      compiler_params=pltpu.CompilerParams(
          dimension_semantics=("arbitrary",), vmem_limit_bytes=VMEM_LIMIT_BYTES),
      name="merge_ffn",
  )(x, yhy, yat, gm, wgate, bgate, why, wat, wout, g2, wg, wu, wd)


@functools.lru_cache(maxsize=None)
def _dft_constants_host():
  f1 = np.arange(DFT_F1)
  n1 = np.arange(DFT_N1)
  f2 = np.arange(DFT_F2)
  n2 = np.arange(DFT_N2)
  eye = np.eye(SUBLANES)
  a_ang = 2.0 * np.pi * (((2 * f1 + 1)[:, None] * n1[None, :]) % DFT_N2) / DFT_N2
  ar, ai = np.cos(a_ang), -np.sin(a_ang)
  mk = np.stack([np.einsum("fn,st->fsnt", ar, eye), np.einsum("fn,st->fsnt", ai, eye)], axis=1)
  mk = mk.reshape(UROWS, KROWS)
  scale = 2.0 / DFT_N
  mki = np.stack([np.einsum("fn,st->ntfs", ar * scale, eye),
                  np.einsum("fn,st->ntfs", ai * scale, eye)], axis=3)
  mki = mki.reshape(KROWS, UROWS)
  ph = (n2[None, None, :] * (DFT_N2 * f2[None, :, None] + 2 * f1[:, None, None] + 1)) % (2 * DFT_N)
  d_ang = 2.0 * np.pi * ph / (2 * DFT_N)
  dr = np.cos(d_ang).reshape(DFT_F1, DFT_F2, DFT_GROUPS, SUBLANES)
  di = -np.sin(d_ang).reshape(DFT_F1, DFT_F2, DFT_GROUPS, SUBLANES)
  top = np.stack([dr, -di], axis=3)
  bot = np.stack([di, dr], axis=3)
  md = np.stack([top, bot], axis=1).reshape(DFT_F1, 2 * DFT_F2, 2 * DFT_N2)
  drt = dr.transpose(0, 2, 3, 1)
  dit = di.transpose(0, 2, 3, 1)
  re_rows = np.stack([drt, dit], axis=3)
  im_rows = np.stack([-dit, drt], axis=3)
  mdi = np.stack([re_rows, im_rows], axis=2).reshape(DFT_F1, 2 * DFT_N2, 2 * DFT_F2)
  return tuple(np.asarray(m, np.float32) for m in (mk, mki, md, mdi))


def _dft_constants():
  return tuple(jnp.asarray(m, dtype=BF16) for m in _dft_constants_host())


def _grouped_loop(count, group, body):
  def outer(i, carry):
    base = i * group
    for k in range(group):
      body(base + k, base, k)
    return carry

  lax.fori_loop(0, count // group, outer, 0)


def _tile_rows(base, scale, offset, size):
  return pl.ds(pl.multiple_of(base * scale, size) + offset, size)


def _stage_k_forward(z_ref, mk_ref, u_ref):
  def body(j, base, k):
    tiles = [z_ref[_tile_rows(base, SUBLANES, DFT_N2 * n1 + SUBLANES * k, SUBLANES), :]
             for n1 in range(DFT_N1)]
    rhs = jnp.concatenate(tiles, axis=0).astype(BF16)
    u_ref[j] = jnp.dot(mk_ref[...], rhs, preferred_element_type=F32).astype(BF16)

  _grouped_loop(DFT_GROUPS, K_UNROLL, body)


def _gather_f1(u_ref, base, k):
  rows = _tile_rows(base, BF16_ROWS, BF16_ROWS * k, BF16_ROWS)
  return jnp.concatenate([u_ref[j, rows, :] for j in range(DFT_GROUPS)], axis=0)


def _scatter_f1(u_ref, base, k, w):
  rows = _tile_rows(base, BF16_ROWS, BF16_ROWS * k, BF16_ROWS)
  wb = w.astype(BF16)
  for j in range(DFT_GROUPS):
    u_ref[j, rows, :] = wb[BF16_ROWS * j:BF16_ROWS * (j + 1), :]


def _spectrum_slot(base, k):
  per_group = UROWS // (2 * DFT_F2)
  return (base // per_group + k // per_group, pl.ds((k % per_group) * 2 * DFT_F2, 2 * DFT_F2),
          slice(None))


def _stage_k_inverse(u_ref, mki_ref, emit):
  def body(j, base, k):
    y = jnp.dot(mki_ref[...], u_ref[j], preferred_element_type=F32)
    for n1 in range(DFT_N1):
      rows = _tile_rows(base, SUBLANES, DFT_N2 * n1 + SUBLANES * k, SUBLANES)
      emit(rows, y[SUBLANES * n1:SUBLANES * (n1 + 1), :])

  _grouped_loop(DFT_GROUPS, K_UNROLL, body)


def _hy_filter_kernel(z_ref, w1_ref, b1_ref, w2_ref, b2_ref, w3f_ref, w3b_ref, dl_ref,
                      mk_ref, md_ref, h_ref, hid_ref, k_ref, u_ref):
  hp = lax.Precision.HIGHEST

  @pl.when((pl.program_id(0) == 0) & (pl.program_id(1) == 0))
  def _():
    hid = jnp.sin(
        jnp.dot(z_ref[...], w1_ref[...], precision=hp, preferred_element_type=F32) + b1_ref[...])
    hid_ref[...] = jnp.sin(
        jnp.dot(hid, w2_ref[...], precision=hp, preferred_element_type=F32) + b2_ref[...])

  hid = hid_ref[...]
  decay = jnp.exp(-z_ref[:, 0:1] * dl_ref[...])
  kf = jnp.dot(hid, w3f_ref[...], precision=hp, preferred_element_type=F32) * decay
  kb = jnp.dot(hid, w3b_ref[...], precision=hp, preferred_element_type=F32) * decay
  row = lax.broadcasted_iota(jnp.int32, kb.shape, 0)
  kb = jnp.where(row == 0, 0.0, kb)
  rs = lax.rsqrt(jnp.sum(kf * kf + kb * kb, axis=0, keepdims=True) + EPS)
  c = kf.shape[1]
  k_ref[:, :c] = kf
  k_ref[:, c:] = kb
  _stage_k_forward(k_ref, mk_ref, u_ref)

  def body(f1, base, k):
    x = jnp.dot(md_ref[f1], _gather_f1(u_ref, base, k), preferred_element_type=F32)
    h_ref[0, f1, pl.ds(0, DFT_F2), :] = (x[:DFT_F2, :c] + x[:DFT_F2, c:]) * rs
    h_ref[0, f1, pl.ds(DFT_F2, DFT_F2), :] = (x[DFT_F2:, :c] - x[DFT_F2:, c:]) * rs

  _grouped_loop(DFT_F1, D_UNROLL, body)


def _hy_filter(zfeat, w1, b1, w2, b2, w3, deltas, mk, md):
  c = HY_CB
  ncb = HY_WIDTH // c
  return pl.pallas_call(
      _hy_filter_kernel,
      grid=(HY_ORDER, ncb),
      in_specs=[_const_spec(zfeat.shape), _const_spec(w1.shape), _const_spec(b1.shape),
                _const_spec(w2.shape), _const_spec(b2.shape),
                pl.BlockSpec((HY_FILT_HIDDEN, c), lambda o, cb: (0, o * 2 * ncb + cb)),
                pl.BlockSpec((HY_FILT_HIDDEN, c), lambda o, cb: (0, o * 2 * ncb + ncb + cb)),
                pl.BlockSpec((1, c), lambda o, cb: (0, cb)),
                _const_spec(mk.shape), _const_spec(md.shape)],
      out_specs=pl.BlockSpec((1, DFT_F1, 2 * DFT_F2, c), lambda o, cb: (o, 0, 0, cb)),
      out_shape=jax.ShapeDtypeStruct((HY_ORDER, DFT_F1, 2 * DFT_F2, HY_WIDTH), F32),
      scratch_shapes=[pltpu.VMEM((SEQ, HY_FILT_HIDDEN), F32),
                      pltpu.VMEM((SEQ, 2 * c), F32),
                      pltpu.VMEM((DFT_GROUPS, UROWS, 2 * c), BF16)],
      compiler_params=pltpu.CompilerParams(
          dimension_semantics=("arbitrary", "arbitrary"), vmem_limit_bytes=VMEM_LIMIT_BYTES),
      name="hy_filter",
  )(zfeat, w1, b1, w2, b2, w3, w3, deltas, mk, md)


def _conv3(p_ref, half, cw_ref, k):
  u = p_ref[pl.ds(half * SEQ, SEQ), :].astype(F32)
  row = lax.broadcasted_iota(jnp.int32, u.shape, 0)
  prev = jnp.where(row == 0, 0.0, pltpu.roll(u, 1, 0))
  nxt = jnp.where(row == SEQ - 1, 0.0, pltpu.roll(u, SEQ - 1, 0))
  w = cw_ref[k]
  return prev * w[0:1] + u * w[1:2] + nxt * w[2:3] + w[3:4]


def _hyena_kernel(pv_ref, pa_ref, pb_ref, cw_ref, skip_ref, h_ref, mk_ref, mki_ref, md_ref,
                  mdi_ref, o_ref, z_ref, y_ref, u_ref, w_ref):
  c = HY_CB
  for half in range(2):
    z_ref[:, half * c:(half + 1) * c] = _conv3(pv_ref, half, cw_ref, 0)
  gates = (pa_ref, pb_ref)
  for order in range(HY_ORDER):
    _stage_k_forward(z_ref, mk_ref, u_ref)

    def spectral(f1, base, k, order=order):
      x = jnp.dot(md_ref[f1], _gather_f1(u_ref, base, k), preferred_element_type=F32)
      xr, xi = x[:DFT_F2], x[DFT_F2:]
      hr = h_ref[order, f1, pl.ds(0, DFT_F2), :]
      hi = h_ref[order, f1, pl.ds(DFT_F2, DFT_F2), :]
      hr = jnp.concatenate([hr, hr], axis=1)
      hi = jnp.concatenate([hi, hi], axis=1)
      y = jnp.concatenate([xr * hr - xi * hi, xr * hi + xi * hr], axis=0).astype(BF16)
      w_ref[_spectrum_slot(base, k)] = y

    _grouped_loop(DFT_F1, D_UNROLL, spectral)

    def inverse_d(f1, base, k):
      w = jnp.dot(mdi_ref[f1], w_ref[_spectrum_slot(base, k)], preferred_element_type=F32)
      _scatter_f1(u_ref, base, k, w)

    _grouped_loop(DFT_F1, D_UNROLL, inverse_d)

    def emit(rows, tile):
      y_ref[rows, :] = tile

    _stage_k_inverse(u_ref, mki_ref, emit)
    for half in range(2):
      lanes = slice(half * c, (half + 1) * c)
      gate = _conv3(gates[order], half, cw_ref, 1 + order)
      z = gate * (y_ref[:, lanes] + skip_ref[order:order + 1, :] * z_ref[:, lanes])
      if order + 1 < HY_ORDER:
        z_ref[:, lanes] = z
      else:
        o_ref[pl.ds(half * SEQ, SEQ), :] = z.astype(BF16)


def _hyena(p, cw, skip, hspec, mk, mki, md, mdi, batch):
  c = HY_CB
  ncb = HY_WIDTH // c
  col = lambda k: pl.BlockSpec((2 * SEQ, c), lambda cb, bp: (bp, k * ncb + cb))
  return pl.pallas_call(
      _hyena_kernel,
      grid=(ncb, batch // 2),
      in_specs=[col(0), col(1), col(2),
                pl.BlockSpec((3, 4, c), lambda cb, bp: (0, 0, cb)),
                pl.BlockSpec((HY_ORDER, c), lambda cb, bp: (0, cb)),
                pl.BlockSpec((HY_ORDER, DFT_F1, 2 * DFT_F2, c), lambda cb, bp: (0, 0, 0, cb),
                             pipeline_mode=pl.Buffered(1)),
                _const_spec(mk.shape), _const_spec(mki.shape), _const_spec(md.shape),
                _const_spec(mdi.shape)],
      out_specs=pl.BlockSpec((2 * SEQ, c), lambda cb, bp: (bp, cb)),
      out_shape=jax.ShapeDtypeStruct((batch * SEQ, HY_WIDTH), BF16),
      scratch_shapes=[pltpu.VMEM((SEQ, 2 * c), F32), pltpu.VMEM((SEQ, 2 * c), F32),
                      pltpu.VMEM((DFT_GROUPS, UROWS, 2 * c), BF16),
                      pltpu.VMEM((DFT_GROUPS, UROWS, 2 * c), BF16)],
      compiler_params=pltpu.CompilerParams(
          dimension_semantics=("arbitrary", "arbitrary"), vmem_limit_bytes=VMEM_LIMIT_BYTES),
      name="hyena",
  )(p, p, p, cw, skip, hspec, mk, mki, md, mdi)


def _t5_bucket(rel):
  half = REL_BUCKETS // 2
  exact = half // 2
  ret = jnp.where(rel > 0, half, 0)
  n = jnp.abs(rel)
  nf = jnp.maximum(n, 1).astype(F32)
  large = exact + (jnp.log(nf / exact) / math.log(REL_MAX_DISTANCE / exact) * (half - exact)).astype(jnp.int32)
  large = jnp.minimum(large, half - 1)
  return ret + jnp.where(n < exact, n, large)


def _bucket_maps():
  qi = jnp.arange(ATT_BQ)[:, None]
  ki = jnp.arange(ATT_BK)[None, :]
  rel = ki - N_SIDE - qi
  maps = [jnp.where(jnp.abs(rel) <= N_SIDE, _t5_bucket(rel * d), -1) for d in DILATIONS]
  return jnp.stack(maps).astype(jnp.int32)


def _dil_attn_kernel(tab_ref, *refs):
  qkv_refs = refs[:9]
  bkt_ref, o_ref = refs[9:11]
  qs_ref, ks_ref, vs_ref, acc_ref, m_ref, l_ref, bias_ref = refs[11:]
  pair = pl.program_id(1)
  lane_lo = lax.broadcasted_iota(jnp.int32, (1, LANES), 1) < HEAD_DIM
  key_idx = lax.broadcasted_iota(jnp.int32, (1, ATT_BK), 1)

  @pl.when((pl.program_id(0) == 0) & (pair == 0))
  def _():
    for g in range(N_GROUPS):
      bkt = bkt_ref[g]
      outside = jnp.where(bkt < 0, NEG, 0.0)
      for c in range(HEADS_PER_GROUP):
        b = outside
        for t in range(REL_BUCKETS):
          b = b + jnp.where(bkt == t, tab_ref[t, g * HEADS_PER_GROUP + c], 0.0)
        bias_ref[g * HEADS_PER_GROUP + c] = b

  for g, d in enumerate(DILATIONS):
    q_ref, k_ref, v_ref = qkv_refs[3 * g:3 * g + 3]
    m_sub = SEQ // d
    pad = N_SIDE * d
    qs_ref[...] = q_ref[...].astype(F32)
    zeros = jnp.zeros((pad, LANES), F32)
    ks_ref[pl.ds(0, pad), :] = zeros
    ks_ref[pl.ds(pad + SEQ, pad), :] = zeros
    vs_ref[pl.ds(0, pad), :] = zeros
    vs_ref[pl.ds(pad + SEQ, pad), :] = zeros
    ks_ref[pl.ds(pad, SEQ), :] = k_ref[...].astype(F32)
    vs_ref[pl.ds(pad, SEQ), :] = v_ref[...].astype(F32)

    def block(t, carry, d=d, m_sub=m_sub, g=g):
      r = t % d
      i = t // d
      start = r + d * ATT_BQ * i
      if d == 1:
        rows_q = pl.ds(start, ATT_BQ)
        rows_k = pl.ds(start, ATT_BK)
      else:
        rows_q = pl.ds(start, ATT_BQ, stride=d)
        rows_k = pl.ds(start, ATT_BK, stride=d)
      q = qs_ref[rows_q, :]
      k = ks_ref[rows_k, :].astype(BF16)
      v = vs_ref[rows_k, :].astype(BF16)
      kpos = ATT_BQ * i - N_SIDE + key_idx
      kvalid = (kpos >= 0) & (kpos < m_sub)
      outs, maxs, sums = [], [], []
      for h in range(2):
        qh = jnp.where(lane_lo if h == 0 else jnp.logical_not(lane_lo), q, 0.0).astype(BF16)
        s = lax.dot_general(qh, k, (((1,), (1,)), ((), ())), preferred_element_type=F32)
        s = jnp.where(kvalid, s + bias_ref[g * HEADS_PER_GROUP + 2 * pair + h], NEG)
        mx = jnp.max(s, axis=-1, keepdims=True)
        p = jnp.exp(s - mx)
        sums.append(jnp.sum(p, axis=-1, keepdims=True))
        maxs.append(mx)
        outs.append(jnp.dot(p.astype(BF16), v, preferred_element_type=F32))
      o = jnp.where(lane_lo, outs[0], outs[1])
      mx = jnp.where(lane_lo, maxs[0], maxs[1])
      sm = jnp.where(lane_lo, sums[0], sums[1])
      if g == 0:
        acc_ref[rows_q, :] = o
        m_ref[rows_q, :] = mx
        l_ref[rows_q, :] = sm
      else:
        m_old = m_ref[rows_q, :]
        m_new = jnp.maximum(m_old, mx)
        a_old = jnp.exp(m_old - m_new)
        a_new = jnp.exp(mx - m_new)
        acc_ref[rows_q, :] = acc_ref[rows_q, :] * a_old + o * a_new
        l_ref[rows_q, :] = l_ref[rows_q, :] * a_old + sm * a_new
        m_ref[rows_q, :] = m_new
      return carry

    lax.fori_loop(0, SEQ // ATT_BQ, block, 0, unroll=ATT_UNROLL)

  o_ref[...] = (acc_ref[...] / l_ref[...]).astype(BF16)


def _dil_attn(p, rel_bias, bkt, batch):
  base = 3 * HY_WIDTH // LANES
  per_kind = ATT_WIDTH // LANES
  per_group = ATT_OUT // LANES

  def col(kind, g):
    return pl.BlockSpec(
        (SEQ, LANES), lambda b, pr, tab: (b, base + kind * per_kind + g * per_group + pr))

  in_specs = [col(kind, g) for g in range(N_GROUPS) for kind in range(3)]
  in_specs += [pl.BlockSpec(bkt.shape, lambda b, pr, tab: (0, 0, 0))]
  grid_spec = pltpu.PrefetchScalarGridSpec(
      num_scalar_prefetch=1,
      grid=(batch, per_group),
      in_specs=in_specs,
      out_specs=pl.BlockSpec((SEQ, LANES), lambda b, pr, tab: (b, pr)),
      scratch_shapes=[pltpu.VMEM((SEQ, LANES), F32),
                      pltpu.VMEM((SEQ + 2 * ATT_PAD, LANES), F32),
                      pltpu.VMEM((SEQ + 2 * ATT_PAD, LANES), F32),
                      pltpu.VMEM((SEQ, LANES), F32), pltpu.VMEM((SEQ, LANES), F32),
                      pltpu.VMEM((SEQ, LANES), F32),
                      pltpu.VMEM((N_GROUPS * HEADS_PER_GROUP, ATT_BQ, ATT_BK), F32)])
  return pl.pallas_call(
      _dil_attn_kernel,
      grid_spec=grid_spec,
      out_shape=jax.ShapeDtypeStruct((batch * SEQ, ATT_OUT), BF16),
      compiler_params=pltpu.CompilerParams(
          dimension_semantics=("arbitrary", "arbitrary"), vmem_limit_bytes=VMEM_LIMIT_BYTES),
      name="dil_attn",
  )(rel_bias, *([p] * 9), bkt)


def _filter_features():
  t = jnp.linspace(0.0, 1.0, SEQ, dtype=F32)[:, None]
  w = (2.0 * math.pi / SEQ) * jnp.arange(SEQ, dtype=F32)[:, None]
  f = jnp.linspace(1e-4, HY_POS_BANDS - 1, HY_POS_BANDS, dtype=F32)[None]
  z = jnp.concatenate([t, jnp.cos(f * w), -jnp.sin(f * w)], axis=-1)
  return jnp.pad(z, ((0, 0), (0, LANES - HY_EMB)))


def _decay_rates():
  max_decay = math.log(HY_DECAY_TARGET) / HY_FAST_DECAY_PCT
  min_decay = math.log(HY_DECAY_TARGET) / HY_SLOW_DECAY_PCT
  return jnp.abs(jnp.linspace(min_decay, max_decay, HY_WIDTH, dtype=F32))[None]


def kernel(x, rel_bias, ffn1_norm, ffn1_w_gate, ffn1_w_up, ffn1_w_down, mix_norm, w_in, w_gate, b_gate, hy_conv_w, hy_conv_b, hy_filt_w1, hy_filt_b1, hy_filt_w2, hy_filt_b2, hy_filt_w3, hy_skip, q_norm, k_norm, w_hy_proj, w_at_proj, w_out, ffn2_norm, ffn2_w_gate, ffn2_w_up, ffn2_w_down):
  batch, seq, d_model = x.shape
  assert (seq, d_model) == (SEQ, D_MODEL) and batch % 2 == 0
  depth = ffn1_norm.shape[0]
  mk, mki, md, mdi = _dft_constants()
  zfeat = _filter_features()
  deltas = _decay_rates()
  bkt = _bucket_maps()
  bf = lambda w: w.astype(BF16)
  row = lambda v: v.reshape(1, -1)

  xt = x.reshape(batch * seq, d_model)
  for l in range(depth):
    xt, p = _ffn_proj(xt, row(ffn1_norm[l]), bf(ffn1_w_gate[l]), bf(ffn1_w_up[l]),
                      bf(ffn1_w_down[l]), row(mix_norm[l]), bf(w_in[l]),
                      row(jnp.tile(q_norm[l], 2)), row(jnp.tile(k_norm[l], 2)))
    w1 = jnp.pad(hy_filt_w1[l], ((0, LANES - HY_EMB), (0, 0)))
    hspec = _hy_filter(zfeat, w1, row(hy_filt_b1[l]), hy_filt_w2[l], row(hy_filt_b2[l]),
                       hy_filt_w3[l], deltas, mk, md)
    cw = jnp.concatenate([hy_conv_w[l], hy_conv_b[l][None]], axis=0)
    cw = cw.reshape(4, 3, HY_WIDTH).transpose(1, 0, 2)
    yhy = _hyena(p, cw, hy_skip[l], hspec, mk, mki, md, mdi, batch)
    yat = _dil_attn(p, rel_bias, bkt, batch)
    xt = _merge_ffn(xt, yhy, yat, row(mix_norm[l]), bf(w_gate[l]), row(b_gate[l]),
                    bf(w_hy_proj[l]), bf(w_at_proj[l]), bf(w_out[l]), row(ffn2_norm[l]),
                    bf(ffn2_w_gate[l]), bf(ffn2_w_up[l]), bf(ffn2_w_down[l]))
  return xt.reshape(batch, seq, d_model)
```

```python
import functools
import math

import jax
import jax.numpy as jnp
import numpy as np
from jax import lax
from jax.experimental import pallas as pl
from jax.experimental.pallas import tpu as pltpu

F32 = jnp.float32
BF16 = jnp.bfloat16

D_MODEL = 1024
SEQ = 4096
HEAD_DIM = 64
HY_WIDTH = 512
HY_ORDER = 2
HY_FILT_HIDDEN = 64
HY_POS_BANDS = 16
HY_EMB = 1 + 2 * HY_POS_BANDS
HY_FAST_DECAY_PCT = 0.3
HY_SLOW_DECAY_PCT = 1.5
HY_DECAY_TARGET = 1e-2
WINDOWS = (128, 512, 2048)
DILATIONS = (1, 4, 16)
N_GROUPS = 3
HEADS_PER_GROUP = 4
ATT_WIDTH = N_GROUPS * HEADS_PER_GROUP * HEAD_DIM
ATT_OUT = HEADS_PER_GROUP * HEAD_DIM
REL_BUCKETS = 32
REL_MAX_DISTANCE = 1024
D_FF = (8 * D_MODEL // 3) // 128 * 128
IN_WIDTH = 3 * HY_WIDTH + 3 * ATT_WIDTH
EPS = 1e-6
NEG = -1e30

SUBLANES = 8
LANES = 128
BF16_ROWS = 16
VMEM_LIMIT_BYTES = 60 * 1024 * 1024

DFT_N = 2 * SEQ
DFT_N2 = 128
DFT_N1 = SEQ // DFT_N2
DFT_F1 = 64
DFT_F2 = SEQ // DFT_F1
DFT_GROUPS = DFT_N2 // SUBLANES
KROWS = DFT_N1 * SUBLANES
UROWS = 2 * DFT_F1 * SUBLANES

N_SIDE = 64
ATT_BQ = 128
ATT_BK = ATT_BQ + 2 * N_SIDE

FFN_CHUNKS = ((0, 1280), (1280, D_FF))
TOKEN_TILE = 512
TOKEN_PARTS = 2
HY_CB = 128
K_UNROLL = 8
D_UNROLL = 16
ATT_UNROLL = 8


def _const_spec(shape):
  nd = len(shape)
  return pl.BlockSpec(shape, lambda *_: (0,) * nd, pipeline_mode=pl.Buffered(1))


def _row_parts(rows):
  part = rows // TOKEN_PARTS
  return [pl.ds(i * part, part) for i in range(TOKEN_PARTS)]


def _rms(x, g):
  return x * lax.rsqrt(jnp.mean(x * x, axis=-1, keepdims=True) + EPS) * g


def _swiglu_residual(x, g_ref, wg_ref, wu_ref, wd_ref):
  h = _rms(x, g_ref[...]).astype(BF16)
  acc = jnp.zeros_like(x)
  for c0, c1 in FFN_CHUNKS:
    a = jnp.dot(h, wg_ref[:, c0:c1], preferred_element_type=F32)
    u = jnp.dot(h, wu_ref[:, c0:c1], preferred_element_type=F32)
    s = (a * jax.nn.sigmoid(a) * u).astype(BF16)
    acc = acc + jnp.dot(s, wd_ref[c0:c1, :], preferred_element_type=F32)
  return x + 0.5 * acc


def _head_norm(x, gain, lane_lo):
  sq = x * x
  s_lo = jnp.sum(jnp.where(lane_lo, sq, 0.0), axis=-1, keepdims=True)
  s_all = jnp.sum(sq, axis=-1, keepdims=True)
  ms = jnp.where(lane_lo, s_lo, s_all - s_lo) * (1.0 / HEAD_DIM)
  return x * lax.rsqrt(ms + EPS) * gain


def _ffn_proj_kernel(x_ref, g1_ref, wg_ref, wu_ref, wd_ref, gm_ref, win_ref, qg_ref, kg_ref,
                     x1_ref, hy_ref, a0_ref, a1_ref, a2_ref, perm_ref):
  lane_lo = lax.broadcasted_iota(jnp.int32, (1, LANES), 1) < HEAD_DIM
  slab_gain = [qg_ref[...] * (HEAD_DIM ** -0.5)] * 2 + [kg_ref[...]] * 2 + [None] * 2
  group_out = (a0_ref, a1_ref, a2_ref)
  for part, rows in enumerate(_row_parts(x_ref.shape[0])):
    x1 = _swiglu_residual(x_ref[rows, :], g1_ref, wg_ref, wu_ref, wd_ref)
    x1_ref[rows, :] = x1
    h = _rms(x1, gm_ref[...]).astype(BF16)
    n_rows = x1.shape[0]
    for c0 in range(0, 3 * HY_WIDTH, ATT_WIDTH):
      hy_ref[rows, c0:c0 + ATT_WIDTH] = jnp.dot(
          h, win_ref[:, c0:c0 + ATT_WIDTH], preferred_element_type=F32).astype(BF16)
    for g, d in enumerate(DILATIONS):
      c0 = 3 * HY_WIDTH + g * ATT_WIDTH
      y = jnp.dot(h, win_ref[:, c0:c0 + ATT_WIDTH], preferred_element_type=F32)
      slabs = []
      for sl, gain in enumerate(slab_gain):
        ys = y[:, sl * LANES:(sl + 1) * LANES]
        slabs.append(ys if gain is None else _head_norm(ys, gain, lane_lo))
      if d == 1:
        a0_ref[rows, :] = jnp.concatenate(slabs, axis=1).astype(BF16)
        continue
      buf = 2 * (g - 1) + part
      for sl, ys in enumerate(slabs):
        perm_ref[buf, sl] = ys
      per_res = n_rows // d
      for r in range(d):
        piece = jnp.concatenate(
            [perm_ref[buf, sl, pl.ds(r, per_res, stride=d), :] for sl in range(len(slabs))], axis=1)
        group_out[g][0, r, pl.ds(part * per_res, per_res), :] = piece.astype(BF16)


def _ffn_proj(x, g1, wg, wu, wd, gm, win, qg, kg, batch):
  t = x.shape[0]
  tm = TOKEN_TILE
  tiles = SEQ // tm
  tok = lambda w: pl.BlockSpec((tm, w), lambda i: (i, 0))
  res = lambda d: pl.BlockSpec((1, d, tm // d, ATT_WIDTH), lambda i: (i // tiles, 0, i % tiles, 0))
  res_shape = lambda d: jax.ShapeDtypeStruct((batch, d, SEQ // d, ATT_WIDTH), BF16)
  d1, d2 = DILATIONS[1], DILATIONS[2]
  return pl.pallas_call(
      _ffn_proj_kernel,
      grid=(t // tm,),
      in_specs=[tok(D_MODEL), _const_spec((1, D_MODEL)), _const_spec((D_MODEL, D_FF)),
                _const_spec((D_MODEL, D_FF)), _const_spec((D_FF, D_MODEL)),
                _const_spec((1, D_MODEL)), _const_spec((D_MODEL, IN_WIDTH)),
                _const_spec((1, LANES)), _const_spec((1, LANES))],
      out_specs=[tok(D_MODEL), tok(3 * HY_WIDTH), tok(ATT_WIDTH), res(d1), res(d2)],
      out_shape=[jax.ShapeDtypeStruct((t, D_MODEL), F32),
                 jax.ShapeDtypeStruct((t, 3 * HY_WIDTH), BF16),
                 jax.ShapeDtypeStruct((t, ATT_WIDTH), BF16), res_shape(d1), res_shape(d2)],
      scratch_shapes=[pltpu.VMEM((2 * TOKEN_PARTS, ATT_WIDTH // LANES, tm // TOKEN_PARTS, LANES),
                                 F32)],
      compiler_params=pltpu.CompilerParams(
          dimension_semantics=("arbitrary",), vmem_limit_bytes=VMEM_LIMIT_BYTES),
      name="ffn_proj",
  )(x, g1, wg, wu, wd, gm, win, qg, kg)


def _merge_ffn_kernel(x_ref, yhy_ref, yat_ref, gm_ref, wgate_ref, bgate_ref, why_ref, wat_ref,
                      wout_ref, g2_ref, wg_ref, wu_ref, wd_ref, o_ref):
  for rows in _row_parts(x_ref.shape[0]):
    x = x_ref[rows, :]
    h = _rms(x, gm_ref[...]).astype(BF16)
    gates = jax.nn.sigmoid(
        jnp.dot(h, wgate_ref[...], preferred_element_type=F32) + bgate_ref[...])
    a = jnp.dot(yhy_ref[rows, :], why_ref[...], preferred_element_type=F32)
    b = jnp.dot(yat_ref[rows, :], wat_ref[...], preferred_element_type=F32)
    y = gates[:, :D_MODEL] * a + gates[:, D_MODEL:] * b
    x2 = x + jnp.dot(y.astype(BF16), wout_ref[...], preferred_element_type=F32)
    o_ref[rows, :] = _swiglu_residual(x2, g2_ref, wg_ref, wu_ref, wd_ref)


def _merge_ffn(x, yhy, yat, gm, wgate, bgate, why, wat, wout, g2, wg, wu, wd):
  t = x.shape[0]
  tm = TOKEN_TILE
  tok = lambda w: pl.BlockSpec((tm, w), lambda i: (i, 0))
  return pl.pallas_call(
      _merge_ffn_kernel,
      grid=(t // tm,),
      in_specs=[tok(D_MODEL), tok(HY_WIDTH), tok(ATT_OUT), _const_spec((1, D_MODEL)),
                _const_spec((D_MODEL, 2 * D_MODEL)), _const_spec((1, 2 * D_MODEL)),
                _const_spec((HY_WIDTH, D_MODEL)), _const_spec((ATT_OUT, D_MODEL)),
                _const_spec((D_MODEL, D_MODEL)), _const_spec((1, D_MODEL)),
                _const_spec((D_MODEL, D_FF)), _const_spec((D_MODEL, D_FF)),
                _const_spec((D_FF, D_MODEL))],
      out_specs=tok(D_MODEL),
      out_shape=jax.ShapeDtypeStruct((t, D_MODEL), F32),
      compiler_params=pltpu.CompilerParams(
          dimension_semantics=("arbitrary",), vmem_limit_bytes=VMEM_LIMIT_BYTES),
      name="merge_ffn",
  )(x, yhy, yat, gm, wgate, bgate, why, wat, wout, g2, wg, wu, wd)


@functools.lru_cache(maxsize=None)
def _dft_constants_host():
  f1 = np.arange(DFT_F1)
  n1 = np.arange(DFT_N1)
  f2 = np.arange(DFT_F2)
  n2 = np.arange(DFT_N2)
  eye = np.eye(SUBLANES)
  a_ang = 2.0 * np.pi * (((2 * f1 + 1)[:, None] * n1[None, :]) % DFT_N2) / DFT_N2
  ar, ai = np.cos(a_ang), -np.sin(a_ang)
  mk = np.stack([np.einsum("fn,st->fsnt", ar, eye), np.einsum("fn,st->fsnt", ai, eye)], axis=1)
  mk = mk.reshape(UROWS, KROWS)
  scale = 2.0 / DFT_N
  mki = np.stack([np.einsum("fn,st->ntfs", ar * scale, eye),
                  np.einsum("fn,st->ntfs", ai * scale, eye)], axis=3)
  mki = mki.reshape(KROWS, UROWS)
  ph = (n2[None, None, :] * (DFT_N2 * f2[None, :, None] + 2 * f1[:, None, None] + 1)) % (2 * DFT_N)
  d_ang = 2.0 * np.pi * ph / (2 * DFT_N)
  dr = np.cos(d_ang).reshape(DFT_F1, DFT_F2, DFT_GROUPS, SUBLANES)
  di = -np.sin(d_ang).reshape(DFT_F1, DFT_F2, DFT_GROUPS, SUBLANES)
  top = np.stack([dr, -di], axis=3)
  bot = np.stack([di, dr], axis=3)
  md = np.stack([top, bot], axis=1).reshape(DFT_F1, 2 * DFT_F2, 2 * DFT_N2)
  drt = dr.transpose(0, 2, 3, 1)
  dit = di.transpose(0, 2, 3, 1)
  re_rows = np.stack([drt, dit], axis=3)
  im_rows = np.stack([-dit, drt], axis=3)
  mdi = np.stack([re_rows, im_rows], axis=2).reshape(DFT_F1, 2 * DFT_N2, 2 * DFT_F2)
  return tuple(np.asarray(m, np.float32) for m in (mk, mki, md, mdi))


def _dft_constants():
  return tuple(jnp.asarray(m, dtype=BF16) for m in _dft_constants_host())


def _grouped_loop(count, group, body):
  def outer(i, carry):
    base = i * group
    for k in range(group):
      body(base + k, base, k)
    return carry

  lax.fori_loop(0, count // group, outer, 0)


def _tile_rows(base, scale, offset, size):
  return pl.ds(pl.multiple_of(base * scale, size) + offset, size)


def _stage_k_forward(z_ref, mk_ref, u_ref):
  def body(j, base, k):
    tiles = [z_ref[_tile_rows(base, SUBLANES, DFT_N2 * n1 + SUBLANES * k, SUBLANES), :]
             for n1 in range(DFT_N1)]
    rhs = jnp.concatenate(tiles, axis=0).astype(BF16)
    u_ref[j] = jnp.dot(mk_ref[...], rhs, preferred_element_type=F32).astype(BF16)

  _grouped_loop(DFT_GROUPS, K_UNROLL, body)


def _gather_f1(u_ref, base, k):
  rows = _tile_rows(base, BF16_ROWS, BF16_ROWS * k, BF16_ROWS)
  return jnp.concatenate([u_ref[j, rows, :] for j in range(DFT_GROUPS)], axis=0)


def _scatter_f1(u_ref, base, k, w):
  rows = _tile_rows(base, BF16_ROWS, BF16_ROWS * k, BF16_ROWS)
  wb = w.astype(BF16)
  for j in range(DFT_GROUPS):
    u_ref[j, rows, :] = wb[BF16_ROWS * j:BF16_ROWS * (j + 1), :]


def _spectrum_slot(base, k):
  per_group = UROWS // (2 * DFT_F2)
  return (base // per_group + k // per_group, pl.ds((k % per_group) * 2 * DFT_F2, 2 * DFT_F2),
          slice(None))


def _stage_k_inverse(u_ref, mki_ref, emit):
  def body(j, base, k):
    y = jnp.dot(mki_ref[...], u_ref[j], preferred_element_type=F32)
    for n1 in range(DFT_N1):
      rows = _tile_rows(base, SUBLANES, DFT_N2 * n1 + SUBLANES * k, SUBLANES)
      emit(rows, y[SUBLANES * n1:SUBLANES * (n1 + 1), :])

  _grouped_loop(DFT_GROUPS, K_UNROLL, body)


def _hy_filter_kernel(z_ref, w1_ref, b1_ref, w2_ref, b2_ref, w3f_ref, w3b_ref, dl_ref,
                      mk_ref, md_ref, h_ref, hid_ref, k_ref, u_ref):
  hp = lax.Precision.HIGHEST

  @pl.when((pl.program_id(0) == 0) & (pl.program_id(1) == 0))
  def _():
    hid = jnp.sin(
        jnp.dot(z_ref[...], w1_ref[...], precision=hp, preferred_element_type=F32) + b1_ref[...])
    hid_ref[...] = jnp.sin(
        jnp.dot(hid, w2_ref[...], precision=hp, preferred_element_type=F32) + b2_ref[...])

  hid = hid_ref[...]
  decay = jnp.exp(-z_ref[:, 0:1] * dl_ref[...])
  kf = jnp.dot(hid, w3f_ref[...], precision=hp, preferred_element_type=F32) * decay
  kb = jnp.dot(hid, w3b_ref[...], precision=hp, preferred_element_type=F32) * decay
  row = lax.broadcasted_iota(jnp.int32, kb.shape, 0)
  kb = jnp.where(row == 0, 0.0, kb)
  rs = lax.rsqrt(jnp.sum(kf * kf + kb * kb, axis=0, keepdims=True) + EPS)
  c = kf.shape[1]
  k_ref[:, :c] = kf
  k_ref[:, c:] = kb
  _stage_k_forward(k_ref, mk_ref, u_ref)

  def body(f1, base, k):
    x = jnp.dot(md_ref[f1], _gather_f1(u_ref, base, k), preferred_element_type=F32)
    h_ref[0, f1, pl.ds(0, DFT_F2), :] = (x[:DFT_F2, :c] + x[:DFT_F2, c:]) * rs
    h_ref[0, f1, pl.ds(DFT_F2, DFT_F2), :] = (x[DFT_F2:, :c] - x[DFT_F2:, c:]) * rs

  _grouped_loop(DFT_F1, D_UNROLL, body)


def _hy_filter(zfeat, w1, b1, w2, b2, w3, deltas, mk, md):
  c = HY_CB
  ncb = HY_WIDTH // c
  return pl.pallas_call(
      _hy_filter_kernel,
      grid=(HY_ORDER, ncb),
      in_specs=[_const_spec(zfeat.shape), _const_spec(w1.shape), _const_spec(b1.shape),
                _const_spec(w2.shape), _const_spec(b2.shape),
                pl.BlockSpec((HY_FILT_HIDDEN, c), lambda o, cb: (0, o * 2 * ncb + cb)),
                pl.BlockSpec((HY_FILT_HIDDEN, c), lambda o, cb: (0, o * 2 * ncb + ncb + cb)),
                pl.BlockSpec((1, c), lambda o, cb: (0, cb)),
                _const_spec(mk.shape), _const_spec(md.shape)],
      out_specs=pl.BlockSpec((1, DFT_F1, 2 * DFT_F2, c), lambda o, cb: (o, 0, 0, cb)),
      out_shape=jax.ShapeDtypeStruct((HY_ORDER, DFT_F1, 2 * DFT_F2, HY_WIDTH), F32),
      scratch_shapes=[pltpu.VMEM((SEQ, HY_FILT_HIDDEN), F32),
                      pltpu.VMEM((SEQ, 2 * c), F32),
                      pltpu.VMEM((DFT_GROUPS, UROWS, 2 * c), BF16)],
      compiler_params=pltpu.CompilerParams(
          dimension_semantics=("arbitrary", "arbitrary"), vmem_limit_bytes=VMEM_LIMIT_BYTES),
      name="hy_filter",
  )(zfeat, w1, b1, w2, b2, w3, w3, deltas, mk, md)


def _conv3(p_ref, half, cw_ref, k):
  u = p_ref[pl.ds(half * SEQ, SEQ), :].astype(F32)
  row = lax.broadcasted_iota(jnp.int32, u.shape, 0)
  prev = jnp.where(row == 0, 0.0, pltpu.roll(u, 1, 0))
  nxt = jnp.where(row == SEQ - 1, 0.0, pltpu.roll(u, SEQ - 1, 0))
  w = cw_ref[k]
  return prev * w[0:1] + u * w[1:2] + nxt * w[2:3] + w[3:4]


def _hyena_kernel(pv_ref, pa_ref, pb_ref, cw_ref, skip_ref, h_ref, mk_ref, mki_ref, md_ref,
                  mdi_ref, o_ref, z_ref, y_ref, u_ref, w_ref):
  c = HY_CB
  for half in range(2):
    z_ref[:, half * c:(half + 1) * c] = _conv3(pv_ref, half, cw_ref, 0)
  gates = (pa_ref, pb_ref)
  for order in range(HY_ORDER):
    _stage_k_forward(z_ref, mk_ref, u_ref)

    def spectral(f1, base, k, order=order):
      x = jnp.dot(md_ref[f1], _gather_f1(u_ref, base, k), preferred_element_type=F32)
      xr, xi = x[:DFT_F2], x[DFT_F2:]
      hr = h_ref[order, f1, pl.ds(0, DFT_F2), :]
      hi = h_ref[order, f1, pl.ds(DFT_F2, DFT_F2), :]
      hr = jnp.concatenate([hr, hr], axis=1)
      hi = jnp.concatenate([hi, hi], axis=1)
      y = jnp.concatenate([xr * hr - xi * hi, xr * hi + xi * hr], axis=0).astype(BF16)
      w_ref[_spectrum_slot(base, k)] = y

    _grouped_loop(DFT_F1, D_UNROLL, spectral)

    def inverse_d(f1, base, k):
      w = jnp.dot(mdi_ref[f1], w_ref[_spectrum_slot(base, k)], preferred_element_type=F32)
      _scatter_f1(u_ref, base, k, w)

    _grouped_loop(DFT_F1, D_UNROLL, inverse_d)

    def emit(rows, tile):
      y_ref[rows, :] = tile

    _stage_k_inverse(u_ref, mki_ref, emit)
    for half in range(2):
      lanes = slice(half * c, (half + 1) * c)
      gate = _conv3(gates[order], half, cw_ref, 1 + order)
      z = gate * (y_ref[:, lanes] + skip_ref[order:order + 1, :] * z_ref[:, lanes])
      if order + 1 < HY_ORDER:
        z_ref[:, lanes] = z
      else:
        o_ref[pl.ds(half * SEQ, SEQ), :] = z.astype(BF16)


def _hyena(p, cw, skip, hspec, mk, mki, md, mdi, batch):
  c = HY_CB
  ncb = HY_WIDTH // c
  col = lambda k: pl.BlockSpec((2 * SEQ, c), lambda cb, bp: (bp, k * ncb + cb))
  return pl.pallas_call(
      _hyena_kernel,
      grid=(ncb, batch // 2),
      in_specs=[col(0), col(1), col(2),
                pl.BlockSpec((3, 4, c), lambda cb, bp: (0, 0, cb)),
                pl.BlockSpec((HY_ORDER, c), lambda cb, bp: (0, cb)),
                pl.BlockSpec((HY_ORDER, DFT_F1, 2 * DFT_F2, c), lambda cb, bp: (0, 0, 0, cb),
                             pipeline_mode=pl.Buffered(1)),
                _const_spec(mk.shape), _const_spec(mki.shape), _const_spec(md.shape),
                _const_spec(mdi.shape)],
      out_specs=pl.BlockSpec((2 * SEQ, c), lambda cb, bp: (bp, cb)),
      out_shape=jax.ShapeDtypeStruct((batch * SEQ, HY_WIDTH), BF16),
      scratch_shapes=[pltpu.VMEM((SEQ, 2 * c), F32), pltpu.VMEM((SEQ, 2 * c), F32),
                      pltpu.VMEM((DFT_GROUPS, UROWS, 2 * c), BF16),
                      pltpu.VMEM((DFT_GROUPS, UROWS, 2 * c), BF16)],
      compiler_params=pltpu.CompilerParams(
          dimension_semantics=("arbitrary", "arbitrary"), vmem_limit_bytes=VMEM_LIMIT_BYTES),
      name="hyena",
  )(p, p, p, cw, skip, hspec, mk, mki, md, mdi)


def _t5_bucket(rel):
  half = REL_BUCKETS // 2
  exact = half // 2
  ret = jnp.where(rel > 0, half, 0)
  n = jnp.abs(rel)
  nf = jnp.maximum(n, 1).astype(F32)
  large = exact + (jnp.log(nf / exact) / math.log(REL_MAX_DISTANCE / exact) * (half - exact)).astype(jnp.int32)
  large = jnp.minimum(large, half - 1)
  return ret + jnp.where(n < exact, n, large)


def _bucket_maps():
  qi = jnp.arange(ATT_BQ)[:, None]
  ki = jnp.arange(ATT_BK)[None, :]
  rel = ki - N_SIDE - qi
  maps = [jnp.where(jnp.abs(rel) <= N_SIDE, _t5_bucket(rel * d), -1) for d in DILATIONS]
  return jnp.stack(maps).astype(jnp.int32)


def _dil_attn_kernel(tab_ref, *refs):
  qkv_refs = refs[:9]
  bkt_ref, o_ref = refs[9:11]
  ks_ref, vs_ref, acc_ref, m_ref, l_ref, bias_ref = refs[11:]
  pair = pl.program_id(1)
  lane_lo = lax.broadcasted_iota(jnp.int32, (1, LANES), 1) < HEAD_DIM
  key_idx = lax.broadcasted_iota(jnp.int32, (1, ATT_BK), 1)

  @pl.when((pl.program_id(0) == 0) & (pair == 0))
  def _():
    for g in range(N_GROUPS):
      bkt = bkt_ref[g]
      outside = jnp.where(bkt < 0, NEG, 0.0)
      for c in range(HEADS_PER_GROUP):
        b = outside
        for t in range(REL_BUCKETS):
          b = b + jnp.where(bkt == t, tab_ref[t, g * HEADS_PER_GROUP + c], 0.0)
        bias_ref[g * HEADS_PER_GROUP + c] = b

  group_order = sorted(range(N_GROUPS), key=lambda g: -DILATIONS[g])
  for g in group_order:
    d = DILATIONS[g]
    q_ref, k_ref, v_ref = qkv_refs[3 * g:3 * g + 3]
    m_sub = SEQ // d
    span = m_sub + 2 * N_SIDE
    blocks = m_sub // ATT_BQ

    def residue_rows(ref, r, start, size, d=d):
      if d == 1:
        return ref[pl.ds(start, size), :]
      return ref[0, r, pl.ds(start, size), :]

    zeros = jnp.zeros((N_SIDE, LANES), BF16)
    for r in range(d):
      for src, dst in ((k_ref, ks_ref), (v_ref, vs_ref)):
        dst[pl.ds(r * span, N_SIDE), :] = zeros
        dst[pl.ds(r * span + N_SIDE, m_sub), :] = residue_rows(src, r, 0, m_sub)
        dst[pl.ds(r * span + N_SIDE + m_sub, N_SIDE), :] = zeros

    def block(t, carry, d=d, m_sub=m_sub, g=g, span=span, blocks=blocks,
              residue_rows=residue_rows, q_ref=q_ref):
      r = t // blocks
      i = t % blocks
      q = residue_rows(q_ref, r, pl.multiple_of(i * ATT_BQ, ATT_BQ), ATT_BQ)
      key_rows = pl.ds(pl.multiple_of(r * span + i * ATT_BQ, ATT_BQ), ATT_BK)
      k = ks_ref[key_rows, :]
      v = vs_ref[key_rows, :]
      start = r + d * ATT_BQ * i
      rows_q = pl.ds(start, ATT_BQ) if d == 1 else pl.ds(start, ATT_BQ, stride=d)
      kpos = ATT_BQ * i - N_SIDE + key_idx
      kvalid = (kpos >= 0) & (kpos < m_sub)
      zero = jnp.zeros_like(q)
      outs, maxs, sums = [], [], []
      for h in range(2):
        qh = jnp.where(lane_lo if h == 0 else jnp.logical_not(lane_lo), q, zero)
        s = lax.dot_general(qh, k, (((1,), (1,)), ((), ())), preferred_element_type=F32)
        s = jnp.where(kvalid, s + bias_ref[g * HEADS_PER_GROUP + 2 * pair + h], NEG)
        mx = jnp.max(s, axis=-1, keepdims=True)
        p = jnp.exp(s - mx)
        sums.append(jnp.sum(p, axis=-1, keepdims=True))
        maxs.append(mx)
        outs.append(jnp.dot(p.astype(BF16), v, preferred_element_type=F32))
      o = jnp.where(lane_lo, outs[0], outs[1])
      mx = jnp.where(lane_lo, maxs[0], maxs[1])
      sm = jnp.where(lane_lo, sums[0], sums[1])
      if g == group_order[0]:
        acc_ref[rows_q, :] = o
        m_ref[rows_q, :] = mx
        l_ref[rows_q, :] = sm
      else:
        m_old = m_ref[rows_q, :]
        m_new = jnp.maximum(m_old, mx)
        a_old = jnp.exp(m_old - m_new)
        a_new = jnp.exp(mx - m_new)
        acc_ref[rows_q, :] = acc_ref[rows_q, :] * a_old + o * a_new
        l_ref[rows_q, :] = l_ref[rows_q, :] * a_old + sm * a_new
        m_ref[rows_q, :] = m_new
      return carry

    lax.fori_loop(0, SEQ // ATT_BQ, block, 0, unroll=ATT_UNROLL)

  o_ref[...] = (acc_ref[...] / l_ref[...]).astype(BF16)


def _dil_attn(groups, rel_bias, bkt, batch):
  per_group = ATT_OUT // LANES

  def col(g, kind):
    d = DILATIONS[g]
    if d == 1:
      return pl.BlockSpec((SEQ, LANES), lambda b, pr, tab: (b, kind * per_group + pr))
    return pl.BlockSpec((1, d, SEQ // d, LANES),
                        lambda b, pr, tab: (b, 0, 0, kind * per_group + pr))

  in_specs = [col(g, kind) for g in range(N_GROUPS) for kind in range(3)]
  in_specs += [pl.BlockSpec(bkt.shape, lambda b, pr, tab: (0, 0, 0))]
  key_rows = SEQ + 2 * N_SIDE * max(DILATIONS)
  grid_spec = pltpu.PrefetchScalarGridSpec(
      num_scalar_prefetch=1,
      grid=(batch, per_group),
      in_specs=in_specs,
      out_specs=pl.BlockSpec((SEQ, LANES), lambda b, pr, tab: (b, pr)),
      scratch_shapes=[pltpu.VMEM((key_rows, LANES), BF16), pltpu.VMEM((key_rows, LANES), BF16),
                      pltpu.VMEM((SEQ, LANES), F32), pltpu.VMEM((SEQ, LANES), F32),
                      pltpu.VMEM((SEQ, LANES), F32),
                      pltpu.VMEM((N_GROUPS * HEADS_PER_GROUP, ATT_BQ, ATT_BK), F32)])
  operands = [groups[g] for g in range(N_GROUPS) for _ in range(3)]
  return pl.pallas_call(
      _dil_attn_kernel,
      grid_spec=grid_spec,
      out_shape=jax.ShapeDtypeStruct((batch * SEQ, ATT_OUT), BF16),
      compiler_params=pltpu.CompilerParams(
          dimension_semantics=("arbitrary", "arbitrary"), vmem_limit_bytes=VMEM_LIMIT_BYTES),
      name="dil_attn",
  )(rel_bias, *operands, bkt)


def _filter_features():
  t = jnp.linspace(0.0, 1.0, SEQ, dtype=F32)[:, None]
  w = (2.0 * math.pi / SEQ) * jnp.arange(SEQ, dtype=F32)[:, None]
  f = jnp.linspace(1e-4, HY_POS_BANDS - 1, HY_POS_BANDS, dtype=F32)[None]
  z = jnp.concatenate([t, jnp.cos(f * w), -jnp.sin(f * w)], axis=-1)
  return jnp.pad(z, ((0, 0), (0, LANES - HY_EMB)))


def _w_in_columns():
  hy = np.arange(3 * HY_WIDTH)
  att = [3 * HY_WIDTH + kind * ATT_WIDTH + g * ATT_OUT + np.arange(ATT_OUT)
         for g in range(N_GROUPS) for kind in range(3)]
  return np.concatenate([hy] + att)


def _decay_rates():
  max_decay = math.log(HY_DECAY_TARGET) / HY_FAST_DECAY_PCT
  min_decay = math.log(HY_DECAY_TARGET) / HY_SLOW_DECAY_PCT
  return jnp.abs(jnp.linspace(min_decay, max_decay, HY_WIDTH, dtype=F32))[None]


def kernel(x, rel_bias, ffn1_norm, ffn1_w_gate, ffn1_w_up, ffn1_w_down, mix_norm, w_in, w_gate, b_gate, hy_conv_w, hy_conv_b, hy_filt_w1, hy_filt_b1, hy_filt_w2, hy_filt_b2, hy_filt_w3, hy_skip, q_norm, k_norm, w_hy_proj, w_at_proj, w_out, ffn2_norm, ffn2_w_gate, ffn2_w_up, ffn2_w_down):
  batch, seq, d_model = x.shape
  assert (seq, d_model) == (SEQ, D_MODEL) and batch % 2 == 0
  depth = ffn1_norm.shape[0]
  mk, mki, md, mdi = _dft_constants()
  zfeat = _filter_features()
  deltas = _decay_rates()
  bkt = _bucket_maps()
  w_in_cols = _w_in_columns()
  bf = lambda w: w.astype(BF16)
  row = lambda v: v.reshape(1, -1)

  xt = x.reshape(batch * seq, d_model)
  for l in range(depth):
    xt, p, *groups = _ffn_proj(
        xt, row(ffn1_norm[l]), bf(ffn1_w_gate[l]), bf(ffn1_w_up[l]), bf(ffn1_w_down[l]),
        row(mix_norm[l]), bf(w_in[l][:, w_in_cols]), row(jnp.tile(q_norm[l], 2)),
        row(jnp.tile(k_norm[l], 2)), batch)
    w1 = jnp.pad(hy_filt_w1[l], ((0, LANES - HY_EMB), (0, 0)))
    hspec = _hy_filter(zfeat, w1, row(hy_filt_b1[l]), hy_filt_w2[l], row(hy_filt_b2[l]),
                       hy_filt_w3[l], deltas, mk, md)
    cw = jnp.concatenate([hy_conv_w[l], hy_conv_b[l][None]], axis=0)
    cw = cw.reshape(4, 3, HY_WIDTH).transpose(1, 0, 2)
    yhy = _hyena(p, cw, hy_skip[l], hspec, mk, mki, md, mdi, batch)
    yat = _dil_attn(groups, rel_bias, bkt, batch)
    xt = _merge_ffn(xt, yhy, yat, row(mix_norm[l]), bf(w_gate[l]), row(b_gate[l]),
                    bf(w_hy_proj[l]), bf(w_at_proj[l]), bf(w_out[l]), row(ffn2_norm[l]),
                    bf(ffn2_w_gate[l]), bf(ffn2_w_up[l]), bf(ffn2_w_down[l]))
  return xt.reshape(batch, seq, d_model)
```

```python
import functools
import math

import jax
import jax.numpy as jnp
import numpy as np
from jax import lax
from jax.experimental import pallas as pl
from jax.experimental.pallas import tpu as pltpu

F32 = jnp.float32
BF16 = jnp.bfloat16

D_MODEL = 1024
SEQ = 4096
HEAD_DIM = 64
HY_WIDTH = 512
HY_ORDER = 2
HY_FILT_HIDDEN = 64
HY_POS_BANDS = 16
HY_EMB = 1 + 2 * HY_POS_BANDS
HY_FAST_DECAY_PCT = 0.3
HY_SLOW_DECAY_PCT = 1.5
HY_DECAY_TARGET = 1e-2
WINDOWS = (128, 512, 2048)
DILATIONS = (1, 4, 16)
N_GROUPS = 3
HEADS_PER_GROUP = 4
ATT_WIDTH = N_GROUPS * HEADS_PER_GROUP * HEAD_DIM
ATT_OUT = HEADS_PER_GROUP * HEAD_DIM
REL_BUCKETS = 32
REL_MAX_DISTANCE = 1024
D_FF = (8 * D_MODEL // 3) // 128 * 128
IN_WIDTH = 3 * HY_WIDTH + 3 * ATT_WIDTH
EPS = 1e-6
NEG = -1e30

SUBLANES = 8
LANES = 128
BF16_ROWS = 16
VMEM_LIMIT_BYTES = 60 * 1024 * 1024

DFT_N = 2 * SEQ
DFT_N2 = 128
DFT_N1 = SEQ // DFT_N2
DFT_F1 = 64
DFT_F2 = SEQ // DFT_F1
DFT_GROUPS = DFT_N2 // SUBLANES
KROWS = DFT_N1 * SUBLANES
UROWS = 2 * DFT_F1 * SUBLANES

N_SIDE = 64
ATT_BQ = 128
ATT_BK = ATT_BQ + 2 * N_SIDE

FFN_BLOCKS = D_FF // LANES
FFN_BLOCK_CHUNKS = ((0, 10), (10, FFN_BLOCKS))
TOKEN_TILE = 512
TOKEN_PARTS = 2
HY_CB = 128
K_UNROLL = 8
D_UNROLL = 16
ATT_UNROLL = 8


def _const_spec(shape):
  nd = len(shape)
  return pl.BlockSpec(shape, lambda *_: (0,) * nd, pipeline_mode=pl.Buffered(1))


def _row_parts(rows):
  part = rows // TOKEN_PARTS
  return [pl.ds(i * part, part) for i in range(TOKEN_PARTS)]


def _rms(x, g):
  return x * lax.rsqrt(jnp.mean(x * x, axis=-1, keepdims=True) + EPS) * g


def _swiglu_residual(x, g_ref, wgu_ref, wd_ref):
  h = _rms(x, g_ref[...]).astype(BF16)
  acc = jnp.zeros_like(x)
  for b0, b1 in FFN_BLOCK_CHUNKS:
    y = jnp.dot(h, wgu_ref[:, 2 * LANES * b0:2 * LANES * b1], preferred_element_type=F32)
    parts = []
    for j in range(b1 - b0):
      a = y[:, 2 * LANES * j:2 * LANES * j + LANES]
      u = y[:, 2 * LANES * j + LANES:2 * LANES * (j + 1)]
      parts.append((a * jax.nn.sigmoid(a) * u).astype(BF16))
    s = jnp.concatenate(parts, axis=1)
    acc = acc + jnp.dot(s, wd_ref[LANES * b0:LANES * b1, :], preferred_element_type=F32)
  return x + 0.5 * acc


def _head_norm(x, gain, lane_lo):
  sq = x * x
  s_lo = jnp.sum(jnp.where(lane_lo, sq, 0.0), axis=-1, keepdims=True)
  s_all = jnp.sum(sq, axis=-1, keepdims=True)
  ms = jnp.where(lane_lo, s_lo, s_all - s_lo) * (1.0 / HEAD_DIM)
  return x * lax.rsqrt(ms + EPS) * gain


def _ffn_proj_kernel(x_ref, g1_ref, wgu_ref, wd_ref, gm_ref, win_ref, qg_ref, kg_ref,
                     x1_ref, hy_ref, a0_ref, a1_ref, a2_ref, perm_ref):
  lane_lo = lax.broadcasted_iota(jnp.int32, (1, LANES), 1) < HEAD_DIM
  kind_gain = (qg_ref[...] * (HEAD_DIM ** -0.5), kg_ref[...], None)
  group_out = (a0_ref, a1_ref, a2_ref)
  for part, rows in enumerate(_row_parts(x_ref.shape[0])):
    x1 = _swiglu_residual(x_ref[rows, :], g1_ref, wgu_ref, wd_ref)
    x1_ref[rows, :] = x1
    h = _rms(x1, gm_ref[...]).astype(BF16)
    n_rows = x1.shape[0]
    for c0 in range(0, 3 * HY_WIDTH, ATT_WIDTH):
      hy_ref[rows, c0:c0 + ATT_WIDTH] = jnp.dot(
          h, win_ref[:, c0:c0 + ATT_WIDTH], preferred_element_type=F32).astype(BF16)
    for g, d in enumerate(DILATIONS):
      slabs = []
      for kind, gain in enumerate(kind_gain):
        c0 = 3 * HY_WIDTH + kind * ATT_WIDTH + g * ATT_OUT
        y = jnp.dot(h, win_ref[:, c0:c0 + ATT_OUT], preferred_element_type=F32)
        for s0 in range(0, ATT_OUT, LANES):
          ys = y[:, s0:s0 + LANES]
          slabs.append(ys if gain is None else _head_norm(ys, gain, lane_lo))
      if d == 1:
        a0_ref[rows, :] = jnp.concatenate(slabs, axis=1).astype(BF16)
        continue
      buf = 2 * (g - 1) + part
      for sl, ys in enumerate(slabs):
        perm_ref[buf, sl] = ys
      per_res = n_rows // d
      for r in range(d):
        piece = jnp.concatenate(
            [perm_ref[buf, sl, pl.ds(r, per_res, stride=d), :] for sl in range(len(slabs))], axis=1)
        group_out[g][0, r, pl.ds(part * per_res, per_res), :] = piece.astype(BF16)


def _ffn_proj(x, g1, wgu, wd, gm, win, qg, kg, batch):
  t = x.shape[0]
  tm = TOKEN_TILE
  tiles = SEQ // tm
  tok = lambda w: pl.BlockSpec((tm, w), lambda i: (i, 0))
  res = lambda d: pl.BlockSpec((1, d, tm // d, ATT_WIDTH), lambda i: (i // tiles, 0, i % tiles, 0))
  res_shape = lambda d: jax.ShapeDtypeStruct((batch, d, SEQ // d, ATT_WIDTH), BF16)
  d1, d2 = DILATIONS[1], DILATIONS[2]
  return pl.pallas_call(
      _ffn_proj_kernel,
      grid=(t // tm,),
      in_specs=[tok(D_MODEL), _const_spec((1, D_MODEL)), _const_spec((D_MODEL, 2 * D_FF)),
                _const_spec((D_FF, D_MODEL)),
                _const_spec((1, D_MODEL)), _const_spec((D_MODEL, IN_WIDTH)),
                _const_spec((1, LANES)), _const_spec((1, LANES))],
      out_specs=[tok(D_MODEL), tok(3 * HY_WIDTH), tok(ATT_WIDTH), res(d1), res(d2)],
      out_shape=[jax.ShapeDtypeStruct((t, D_MODEL), F32),
                 jax.ShapeDtypeStruct((t, 3 * HY_WIDTH), BF16),
                 jax.ShapeDtypeStruct((t, ATT_WIDTH), BF16), res_shape(d1), res_shape(d2)],
      scratch_shapes=[pltpu.VMEM((2 * TOKEN_PARTS, ATT_WIDTH // LANES, tm // TOKEN_PARTS, LANES),
                                 F32)],
      compiler_params=pltpu.CompilerParams(
          dimension_semantics=("arbitrary",), vmem_limit_bytes=VMEM_LIMIT_BYTES),
      name="ffn_proj",
  )(x, g1, wgu, wd, gm, win, qg, kg)


def _merge_ffn_kernel(x_ref, yhy_ref, yat_ref, gm_ref, wgate_ref, bgate_ref, why_ref, wat_ref,
                      wout_ref, g2_ref, wgu_ref, wd_ref, o_ref):
  for rows in _row_parts(x_ref.shape[0]):
    x = x_ref[rows, :]
    h = _rms(x, gm_ref[...]).astype(BF16)
    gates = jax.nn.sigmoid(
        jnp.dot(h, wgate_ref[...], preferred_element_type=F32) + bgate_ref[...])
    a = jnp.dot(yhy_ref[rows, :], why_ref[...], preferred_element_type=F32)
    b = jnp.dot(yat_ref[rows, :], wat_ref[...], preferred_element_type=F32)
    y = gates[:, :D_MODEL] * a + gates[:, D_MODEL:] * b
    x2 = x + jnp.dot(y.astype(BF16), wout_ref[...], preferred_element_type=F32)
    o_ref[rows, :] = _swiglu_residual(x2, g2_ref, wgu_ref, wd_ref)


def _merge_ffn(x, yhy, yat, gm, wgate, bgate, why, wat, wout, g2, wgu, wd):
  t = x.shape[0]
  tm = TOKEN_TILE
  tok = lambda w: pl.BlockSpec((tm, w), lambda i: (i, 0))
  return pl.pallas_call(
      _merge_ffn_kernel,
      grid=(t // tm,),
      in_specs=[tok(D_MODEL), tok(HY_WIDTH), tok(ATT_OUT), _const_spec((1, D_MODEL)),
                _const_spec((D_MODEL, 2 * D_MODEL)), _const_spec((1, 2 * D_MODEL)),
                _const_spec((HY_WIDTH, D_MODEL)), _const_spec((ATT_OUT, D_MODEL)),
                _const_spec((D_MODEL, D_MODEL)), _const_spec((1, D_MODEL)),
                _const_spec((D_MODEL, 2 * D_FF)), _const_spec((D_FF, D_MODEL))],
      out_specs=tok(D_MODEL),
      out_shape=jax.ShapeDtypeStruct((t, D_MODEL), F32),
      compiler_params=pltpu.CompilerParams(
          dimension_semantics=("arbitrary",), vmem_limit_bytes=VMEM_LIMIT_BYTES),
      name="merge_ffn",
  )(x, yhy, yat, gm, wgate, bgate, why, wat, wout, g2, wgu, wd)


@functools.lru_cache(maxsize=None)
def _dft_constants_host():
  f1 = np.arange(DFT_F1)
  n1 = np.arange(DFT_N1)
  f2 = np.arange(DFT_F2)
  n2 = np.arange(DFT_N2)
  eye = np.eye(SUBLANES)
  a_ang = 2.0 * np.pi * (((2 * f1 + 1)[:, None] * n1[None, :]) % DFT_N2) / DFT_N2
  ar, ai = np.cos(a_ang), -np.sin(a_ang)
  mk = np.stack([np.einsum("fn,st->fsnt", ar, eye), np.einsum("fn,st->fsnt", ai, eye)], axis=1)
  mk = mk.reshape(UROWS, KROWS)
  scale = 2.0 / DFT_N
  mki = np.stack([np.einsum("fn,st->ntfs", ar * scale, eye),
                  np.einsum("fn,st->ntfs", ai * scale, eye)], axis=3)
  mki = mki.reshape(KROWS, UROWS)
  ph = (n2[None, None, :] * (DFT_N2 * f2[None, :, None] + 2 * f1[:, None, None] + 1)) % (2 * DFT_N)
  d_ang = 2.0 * np.pi * ph / (2 * DFT_N)
  dr = np.cos(d_ang).reshape(DFT_F1, DFT_F2, DFT_GROUPS, SUBLANES)
  di = -np.sin(d_ang).reshape(DFT_F1, DFT_F2, DFT_GROUPS, SUBLANES)
  top = np.stack([dr, -di], axis=3)
  bot = np.stack([di, dr], axis=3)
  md = np.stack([top, bot], axis=1).reshape(DFT_F1, 2 * DFT_F2, 2 * DFT_N2)
  drt = dr.transpose(0, 2, 3, 1)
  dit = di.transpose(0, 2, 3, 1)
  re_rows = np.stack([drt, dit], axis=3)
  im_rows = np.stack([-dit, drt], axis=3)
  mdi = np.stack([re_rows, im_rows], axis=2).reshape(DFT_F1, 2 * DFT_N2, 2 * DFT_F2)
  return tuple(np.asarray(m, np.float32) for m in (mk, mki, md, mdi))


def _dft_constants():
  return tuple(jnp.asarray(m, dtype=BF16) for m in _dft_constants_host())


def _grouped_loop(count, group, body):
  def outer(i, carry):
    base = i * group
    for k in range(group):
      body(base + k, base, k)
    return carry

  lax.fori_loop(0, count // group, outer, 0)


def _tile_rows(base, scale, offset, size):
  return pl.ds(pl.multiple_of(base * scale, size) + offset, size)


def _stage_k_forward(z_ref, mk_ref, u_ref):
  def body(j, base, k):
    tiles = [z_ref[_tile_rows(base, SUBLANES, DFT_N2 * n1 + SUBLANES * k, SUBLANES), :]
             for n1 in range(DFT_N1)]
    rhs = jnp.concatenate(tiles, axis=0).astype(BF16)
    u_ref[j] = jnp.dot(mk_ref[...], rhs, preferred_element_type=F32).astype(BF16)

  _grouped_loop(DFT_GROUPS, K_UNROLL, body)


def _gather_f1(u_ref, base, k):
  rows = _tile_rows(base, BF16_ROWS, BF16_ROWS * k, BF16_ROWS)
  return jnp.concatenate([u_ref[j, rows, :] for j in range(DFT_GROUPS)], axis=0)


def _scatter_f1(u_ref, base, k, w):
  rows = _tile_rows(base, BF16_ROWS, BF16_ROWS * k, BF16_ROWS)
  wb = w.astype(BF16)
  for j in range(DFT_GROUPS):
    u_ref[j, rows, :] = wb[BF16_ROWS * j:BF16_ROWS * (j + 1), :]


def _spectrum_slot(base, k):
  per_group = UROWS // (2 * DFT_F2)
  return (base // per_group + k // per_group, pl.ds((k % per_group) * 2 * DFT_F2, 2 * DFT_F2),
          slice(None))


def _stage_k_inverse(u_ref, mki_ref, emit):
  def body(j, base, k):
    y = jnp.dot(mki_ref[...], u_ref[j], preferred_element_type=F32)
    for n1 in range(DFT_N1):
      rows = _tile_rows(base, SUBLANES, DFT_N2 * n1 + SUBLANES * k, SUBLANES)
      emit(rows, y[SUBLANES * n1:SUBLANES * (n1 + 1), :])

  _grouped_loop(DFT_GROUPS, K_UNROLL, body)


def _hy_filter_kernel(z_ref, w1_ref, b1_ref, w2_ref, b2_ref, w3f_ref, w3b_ref, dl_ref,
                      mk_ref, md_ref, h_ref, hid_ref, k_ref, u_ref):
  hp = lax.Precision.HIGHEST

  @pl.when((pl.program_id(0) == 0) & (pl.program_id(1) == 0))
  def _():
    hid = jnp.sin(
        jnp.dot(z_ref[...], w1_ref[...], precision=hp, preferred_element_type=F32) + b1_ref[...])
    hid_ref[...] = jnp.sin(
        jnp.dot(hid, w2_ref[...], precision=hp, preferred_element_type=F32) + b2_ref[...])

  hid = hid_ref[...]
  decay = jnp.exp(-z_ref[:, 0:1] * dl_ref[...])
  kf = jnp.dot(hid, w3f_ref[...], precision=hp, preferred_element_type=F32) * decay
  kb = jnp.dot(hid, w3b_ref[...], precision=hp, preferred_element_type=F32) * decay
  row = lax.broadcasted_iota(jnp.int32, kb.shape, 0)
  kb = jnp.where(row == 0, 0.0, kb)
  rs = lax.rsqrt(jnp.sum(kf * kf + kb * kb, axis=0, keepdims=True) + EPS)
  c = kf.shape[1]
  k_ref[:, :c] = kf
  k_ref[:, c:] = kb
  _stage_k_forward(k_ref, mk_ref, u_ref)

  def body(f1, base, k):
    x = jnp.dot(md_ref[f1], _gather_f1(u_ref, base, k), preferred_element_type=F32)
    h_ref[0, f1, pl.ds(0, DFT_F2), :] = (x[:DFT_F2, :c] + x[:DFT_F2, c:]) * rs
    h_ref[0, f1, pl.ds(DFT_F2, DFT_F2), :] = (x[DFT_F2:, :c] - x[DFT_F2:, c:]) * rs

  _grouped_loop(DFT_F1, D_UNROLL, body)


def _hy_filter(zfeat, w1, b1, w2, b2, w3, deltas, mk, md):
  c = HY_CB
  ncb = HY_WIDTH // c
  return pl.pallas_call(
      _hy_filter_kernel,
      grid=(HY_ORDER, ncb),
      in_specs=[_const_spec(zfeat.shape), _const_spec(w1.shape), _const_spec(b1.shape),
                _const_spec(w2.shape), _const_spec(b2.shape),
                pl.BlockSpec((HY_FILT_HIDDEN, c), lambda o, cb: (0, o * 2 * ncb + cb)),
                pl.BlockSpec((HY_FILT_HIDDEN, c), lambda o, cb: (0, o * 2 * ncb + ncb + cb)),
                pl.BlockSpec((1, c), lambda o, cb: (0, cb)),
                _const_spec(mk.shape), _const_spec(md.shape)],
      out_specs=pl.BlockSpec((1, DFT_F1, 2 * DFT_F2, c), lambda o, cb: (o, 0, 0, cb)),
      out_shape=jax.ShapeDtypeStruct((HY_ORDER, DFT_F1, 2 * DFT_F2, HY_WIDTH), F32),
      scratch_shapes=[pltpu.VMEM((SEQ, HY_FILT_HIDDEN), F32),
                      pltpu.VMEM((SEQ, 2 * c), F32),
                      pltpu.VMEM((DFT_GROUPS, UROWS, 2 * c), BF16)],
      compiler_params=pltpu.CompilerParams(
          dimension_semantics=("arbitrary", "arbitrary"), vmem_limit_bytes=VMEM_LIMIT_BYTES),
      name="hy_filter",
  )(zfeat, w1, b1, w2, b2, w3, w3, deltas, mk, md)


def _conv3(p_ref, half, cw_ref, k):
  u = p_ref[pl.ds(half * SEQ, SEQ), :].astype(F32)
  row = lax.broadcasted_iota(jnp.int32, u.shape, 0)
  prev = jnp.where(row == 0, 0.0, pltpu.roll(u, 1, 0))
  nxt = jnp.where(row == SEQ - 1, 0.0, pltpu.roll(u, SEQ - 1, 0))
  w = cw_ref[k]
  return prev * w[0:1] + u * w[1:2] + nxt * w[2:3] + w[3:4]


def _hyena_kernel(pv_ref, pa_ref, pb_ref, cw_ref, skip_ref, h_ref, mk_ref, mki_ref, md_ref,
                  mdi_ref, o_ref, z_ref, y_ref, u_ref, w_ref):
  c = HY_CB
  for half in range(2):
    z_ref[:, half * c:(half + 1) * c] = _conv3(pv_ref, half, cw_ref, 0)
  gates = (pa_ref, pb_ref)
  for order in range(HY_ORDER):
    _stage_k_forward(z_ref, mk_ref, u_ref)

    def spectral(f1, base, k, order=order):
      x = jnp.dot(md_ref[f1], _gather_f1(u_ref, base, k), preferred_element_type=F32)
      xr, xi = x[:DFT_F2], x[DFT_F2:]
      hr = h_ref[order, f1, pl.ds(0, DFT_F2), :]
      hi = h_ref[order, f1, pl.ds(DFT_F2, DFT_F2), :]
      hr = jnp.concatenate([hr, hr], axis=1)
      hi = jnp.concatenate([hi, hi], axis=1)
      y = jnp.concatenate([xr * hr - xi * hi, xr * hi + xi * hr], axis=0).astype(BF16)
      w_ref[_spectrum_slot(base, k)] = y

    _grouped_loop(DFT_F1, D_UNROLL, spectral)

    def inverse_d(f1, base, k):
      w = jnp.dot(mdi_ref[f1], w_ref[_spectrum_slot(base, k)], preferred_element_type=F32)
      _scatter_f1(u_ref, base, k, w)

    _grouped_loop(DFT_F1, D_UNROLL, inverse_d)

    def emit(rows, tile):
      y_ref[rows, :] = tile

    _stage_k_inverse(u_ref, mki_ref, emit)
    for half in range(2):
      lanes = slice(half * c, (half + 1) * c)
      gate = _conv3(gates[order], half, cw_ref, 1 + order)
      z = gate * (y_ref[:, lanes] + skip_ref[order:order + 1, :] * z_ref[:, lanes])
      if order + 1 < HY_ORDER:
        z_ref[:, lanes] = z
      else:
        o_ref[pl.ds(half * SEQ, SEQ), :] = z.astype(BF16)


def _hyena(p, cw, skip, hspec, mk, mki, md, mdi, batch):
  c = HY_CB
  ncb = HY_WIDTH // c
  col = lambda k: pl.BlockSpec((2 * SEQ, c), lambda cb, bp: (bp, k * ncb + cb))
  return pl.pallas_call(
      _hyena_kernel,
      grid=(ncb, batch // 2),
      in_specs=[col(0), col(1), col(2),
                pl.BlockSpec((3, 4, c), lambda cb, bp: (0, 0, cb)),
                pl.BlockSpec((HY_ORDER, c), lambda cb, bp: (0, cb)),
                pl.BlockSpec((HY_ORDER, DFT_F1, 2 * DFT_F2, c), lambda cb, bp: (0, 0, 0, cb),
                             pipeline_mode=pl.Buffered(1)),
                _const_spec(mk.shape), _const_spec(mki.shape), _const_spec(md.shape),
                _const_spec(mdi.shape)],
      out_specs=pl.BlockSpec((2 * SEQ, c), lambda cb, bp: (bp, cb)),
      out_shape=jax.ShapeDtypeStruct((batch * SEQ, HY_WIDTH), BF16),
      scratch_shapes=[pltpu.VMEM((SEQ, 2 * c), F32), pltpu.VMEM((SEQ, 2 * c), F32),
                      pltpu.VMEM((DFT_GROUPS, UROWS, 2 * c), BF16),
                      pltpu.VMEM((DFT_GROUPS, UROWS, 2 * c), BF16)],
      compiler_params=pltpu.CompilerParams(
          dimension_semantics=("arbitrary", "arbitrary"), vmem_limit_bytes=VMEM_LIMIT_BYTES),
      name="hyena",
  )(p, p, p, cw, skip, hspec, mk, mki, md, mdi)


def _t5_bucket(rel):
  half = REL_BUCKETS // 2
  exact = half // 2
  ret = jnp.where(rel > 0, half, 0)
  n = jnp.abs(rel)
  nf = jnp.maximum(n, 1).astype(F32)
  large = exact + (jnp.log(nf / exact) / math.log(REL_MAX_DISTANCE / exact) * (half - exact)).astype(jnp.int32)
  large = jnp.minimum(large, half - 1)
  return ret + jnp.where(n < exact, n, large)


def _bucket_maps():
  qi = jnp.arange(ATT_BQ)[:, None]
  ki = jnp.arange(ATT_BK)[None, :]
  rel = ki - N_SIDE - qi
  maps = [jnp.where(jnp.abs(rel) <= N_SIDE, _t5_bucket(rel * d), -1) for d in DILATIONS]
  return jnp.stack(maps).astype(jnp.int32)


def _dil_attn_kernel(tab_ref, *refs):
  qkv_refs = refs[:9]
  bkt_ref, o_ref = refs[9:11]
  ks_ref, vs_ref, acc_ref, m_ref, l_ref, bias_ref = refs[11:]
  pair = pl.program_id(1)
  lane_lo = lax.broadcasted_iota(jnp.int32, (1, LANES), 1) < HEAD_DIM
  key_idx = lax.broadcasted_iota(jnp.int32, (1, ATT_BK), 1)

  @pl.when((pl.program_id(0) == 0) & (pair == 0))
  def _():
    for g in range(N_GROUPS):
      bkt = bkt_ref[g]
      outside = jnp.where(bkt < 0, NEG, 0.0)
      for c in range(HEADS_PER_GROUP):
        b = outside
        for t in range(REL_BUCKETS):
          b = b + jnp.where(bkt == t, tab_ref[t, g * HEADS_PER_GROUP + c], 0.0)
        bias_ref[g * HEADS_PER_GROUP + c] = b

  group_order = sorted(range(N_GROUPS), key=lambda g: -DILATIONS[g])
  for g in group_order:
    d = DILATIONS[g]
    q_ref, k_ref, v_ref = qkv_refs[3 * g:3 * g + 3]
    m_sub = SEQ // d
    span = m_sub + 2 * N_SIDE
    blocks = m_sub // ATT_BQ

    def residue_rows(ref, r, start, size, d=d):
      if d == 1:
        return ref[pl.ds(start, size), :]
      return ref[0, r, pl.ds(start, size), :]

    zeros = jnp.zeros((N_SIDE, LANES), BF16)
    for r in range(d):
      for src, dst in ((k_ref, ks_ref), (v_ref, vs_ref)):
        dst[pl.ds(r * span, N_SIDE), :] = zeros
        dst[pl.ds(r * span + N_SIDE, m_sub), :] = residue_rows(src, r, 0, m_sub)
        dst[pl.ds(r * span + N_SIDE + m_sub, N_SIDE), :] = zeros

    def block(t, carry, d=d, m_sub=m_sub, g=g, span=span, blocks=blocks,
              residue_rows=residue_rows, q_ref=q_ref):
      r = t // blocks
      i = t % blocks
      q = residue_rows(q_ref, r, pl.multiple_of(i * ATT_BQ, ATT_BQ), ATT_BQ)
      key_rows = pl.ds(pl.multiple_of(r * span + i * ATT_BQ, ATT_BQ), ATT_BK)
      k = ks_ref[key_rows, :]
      v = vs_ref[key_rows, :]
      start = r + d * ATT_BQ * i
      rows_q = pl.ds(start, ATT_BQ) if d == 1 else pl.ds(start, ATT_BQ, stride=d)
      kpos = ATT_BQ * i - N_SIDE + key_idx
      kvalid = (kpos >= 0) & (kpos < m_sub)
      zero = jnp.zeros_like(q)
      outs, maxs, sums = [], [], []
      for h in range(2):
        qh = jnp.where(lane_lo if h == 0 else jnp.logical_not(lane_lo), q, zero)
        s = lax.dot_general(qh, k, (((1,), (1,)), ((), ())), preferred_element_type=F32)
        s = jnp.where(kvalid, s + bias_ref[g * HEADS_PER_GROUP + 2 * pair + h], NEG)
        mx = jnp.max(s, axis=-1, keepdims=True)
        p = jnp.exp(s - mx)
        sums.append(jnp.sum(p, axis=-1, keepdims=True))
        maxs.append(mx)
        outs.append(jnp.dot(p.astype(BF16), v, preferred_element_type=F32))
      o = jnp.where(lane_lo, outs[0], outs[1])
      mx = jnp.where(lane_lo, maxs[0], maxs[1])
      sm = jnp.where(lane_lo, sums[0], sums[1])
      if g == group_order[0]:
        acc_ref[rows_q, :] = o
        m_ref[rows_q, :] = mx
        l_ref[rows_q, :] = sm
      else:
        m_old = m_ref[rows_q, :]
        m_new = jnp.maximum(m_old, mx)
        a_old = jnp.exp(m_old - m_new)
        a_new = jnp.exp(mx - m_new)
        acc_ref[rows_q, :] = acc_ref[rows_q, :] * a_old + o * a_new
        l_ref[rows_q, :] = l_ref[rows_q, :] * a_old + sm * a_new
        m_ref[rows_q, :] = m_new
      return carry

    lax.fori_loop(0, SEQ // ATT_BQ, block, 0, unroll=ATT_UNROLL)

  o_ref[...] = (acc_ref[...] / l_ref[...]).astype(BF16)


def _dil_attn(groups, rel_bias, bkt, batch):
  per_group = ATT_OUT // LANES

  def col(g, kind):
    d = DILATIONS[g]
    if d == 1:
      return pl.BlockSpec((SEQ, LANES), lambda b, pr, tab: (b, kind * per_group + pr))
    return pl.BlockSpec((1, d, SEQ // d, LANES),
                        lambda b, pr, tab: (b, 0, 0, kind * per_group + pr))

  in_specs = [col(g, kind) for g in range(N_GROUPS) for kind in range(3)]
  in_specs += [pl.BlockSpec(bkt.shape, lambda b, pr, tab: (0, 0, 0))]
  key_rows = SEQ + 2 * N_SIDE * max(DILATIONS)
  grid_spec = pltpu.PrefetchScalarGridSpec(
      num_scalar_prefetch=1,
      grid=(batch, per_group),
      in_specs=in_specs,
      out_specs=pl.BlockSpec((SEQ, LANES), lambda b, pr, tab: (b, pr)),
      scratch_shapes=[pltpu.VMEM((key_rows, LANES), BF16), pltpu.VMEM((key_rows, LANES), BF16),
                      pltpu.VMEM((SEQ, LANES), F32), pltpu.VMEM((SEQ, LANES), F32),
                      pltpu.VMEM((SEQ, LANES), F32),
                      pltpu.VMEM((N_GROUPS * HEADS_PER_GROUP, ATT_BQ, ATT_BK), F32)])
  operands = [groups[g] for g in range(N_GROUPS) for _ in range(3)]
  return pl.pallas_call(
      _dil_attn_kernel,
      grid_spec=grid_spec,
      out_shape=jax.ShapeDtypeStruct((batch * SEQ, ATT_OUT), BF16),
      compiler_params=pltpu.CompilerParams(
          dimension_semantics=("arbitrary", "arbitrary"), vmem_limit_bytes=VMEM_LIMIT_BYTES),
      name="dil_attn",
  )(rel_bias, *operands, bkt)


def _filter_features():
  t = jnp.linspace(0.0, 1.0, SEQ, dtype=F32)[:, None]
  w = (2.0 * math.pi / SEQ) * jnp.arange(SEQ, dtype=F32)[:, None]
  f = jnp.linspace(1e-4, HY_POS_BANDS - 1, HY_POS_BANDS, dtype=F32)[None]
  z = jnp.concatenate([t, jnp.cos(f * w), -jnp.sin(f * w)], axis=-1)
  return jnp.pad(z, ((0, 0), (0, LANES - HY_EMB)))


def _gate_up(w_gate, w_up):
  both = jnp.stack([w_gate.reshape(D_MODEL, FFN_BLOCKS, LANES),
                    w_up.reshape(D_MODEL, FFN_BLOCKS, LANES)], axis=2)
  return both.reshape(D_MODEL, 2 * D_FF).astype(BF16)


def _decay_rates():
  max_decay = math.log(HY_DECAY_TARGET) / HY_FAST_DECAY_PCT
  min_decay = math.log(HY_DECAY_TARGET) / HY_SLOW_DECAY_PCT
  return jnp.abs(jnp.linspace(min_decay, max_decay, HY_WIDTH, dtype=F32))[None]


def kernel(x, rel_bias, ffn1_norm, ffn1_w_gate, ffn1_w_up, ffn1_w_down, mix_norm, w_in, w_gate, b_gate, hy_conv_w, hy_conv_b, hy_filt_w1, hy_filt_b1, hy_filt_w2, hy_filt_b2, hy_filt_w3, hy_skip, q_norm, k_norm, w_hy_proj, w_at_proj, w_out, ffn2_norm, ffn2_w_gate, ffn2_w_up, ffn2_w_down):
  batch, seq, d_model = x.shape
  assert (seq, d_model) == (SEQ, D_MODEL) and batch % 2 == 0
  depth = ffn1_norm.shape[0]
  mk, mki, md, mdi = _dft_constants()
  zfeat = _filter_features()
  deltas = _decay_rates()
  bkt = _bucket_maps()
  bf = lambda w: w.astype(BF16)
  row = lambda v: v.reshape(1, -1)

  xt = x.reshape(batch * seq, d_model)
  for l in range(depth):
    xt, p, *groups = _ffn_proj(
        xt, row(ffn1_norm[l]), _gate_up(ffn1_w_gate[l], ffn1_w_up[l]), bf(ffn1_w_down[l]),
        row(mix_norm[l]), bf(w_in[l]), row(jnp.tile(q_norm[l], 2)),
        row(jnp.tile(k_norm[l], 2)), batch)
    w1 = jnp.pad(hy_filt_w1[l], ((0, LANES - HY_EMB), (0, 0)))
    hspec = _hy_filter(zfeat, w1, row(hy_filt_b1[l]), hy_filt_w2[l], row(hy_filt_b2[l]),
                       hy_filt_w3[l], deltas, mk, md)
    cw = jnp.concatenate([hy_conv_w[l], hy_conv_b[l][None]], axis=0)
    cw = cw.reshape(4, 3, HY_WIDTH).transpose(1, 0, 2)
    yhy = _hyena(p, cw, hy_skip[l], hspec, mk, mki, md, mdi, batch)
    yat = _dil_attn(groups, rel_bias, bkt, batch)
    xt = _merge_ffn(xt, yhy, yat, row(mix_norm[l]), bf(w_gate[l]), row(b_gate[l]),
                    bf(w_hy_proj[l]), bf(w_at_proj[l]), bf(w_out[l]), row(ffn2_norm[l]),
                    _gate_up(ffn2_w_gate[l], ffn2_w_up[l]), bf(ffn2_w_down[l]))
  return xt.reshape(batch, seq, d_model)
```

```python
import functools
import math

import jax
import jax.numpy as jnp
import numpy as np
from jax import lax
from jax.experimental import pallas as pl
from jax.experimental.pallas import tpu as pltpu

F32 = jnp.float32
BF16 = jnp.bfloat16

D_MODEL = 1024
SEQ = 4096
HEAD_DIM = 64
HY_WIDTH = 512
HY_ORDER = 2
HY_FILT_HIDDEN = 64
HY_POS_BANDS = 16
HY_EMB = 1 + 2 * HY_POS_BANDS
HY_FAST_DECAY_PCT = 0.3
HY_SLOW_DECAY_PCT = 1.5
HY_DECAY_TARGET = 1e-2
WINDOWS = (128, 512, 2048)
DILATIONS = (1, 4, 16)
N_GROUPS = 3
HEADS_PER_GROUP = 4
ATT_WIDTH = N_GROUPS * HEADS_PER_GROUP * HEAD_DIM
ATT_OUT = HEADS_PER_GROUP * HEAD_DIM
REL_BUCKETS = 32
REL_MAX_DISTANCE = 1024
D_FF = (8 * D_MODEL // 3) // 128 * 128
IN_WIDTH = 3 * HY_WIDTH + 3 * ATT_WIDTH
EPS = 1e-6
NEG = -1e30

SUBLANES = 8
LANES = 128
BF16_ROWS = 16
VMEM_LIMIT_BYTES = 60 * 1024 * 1024

DFT_N = 2 * SEQ
DFT_N2 = 128
DFT_N1 = SEQ // DFT_N2
DFT_F1 = 64
DFT_F2 = SEQ // DFT_F1
DFT_GROUPS = DFT_N2 // SUBLANES
KROWS = DFT_N1 * SUBLANES
UROWS = 2 * DFT_F1 * SUBLANES

N_SIDE = 64
ATT_BQ = 128
ATT_BK = ATT_BQ + 2 * N_SIDE

FFN_MAIN = D_FF // 256 * 256
FFN_CHUNKS = ((0, FFN_MAIN // 2), (FFN_MAIN // 2, FFN_MAIN))
TOKEN_TILE = 512
TOKEN_PARTS = 2
HY_CB = 128
K_UNROLL = 8
D_UNROLL = 16
ATT_UNROLL = 8


def _const_spec(shape):
  nd = len(shape)
  return pl.BlockSpec(shape, lambda *_: (0,) * nd, pipeline_mode=pl.Buffered(1))


def _row_parts(rows):
  part = rows // TOKEN_PARTS
  return [pl.ds(i * part, part) for i in range(TOKEN_PARTS)]


def _rms(x, g):
  return x * lax.rsqrt(jnp.mean(x * x, axis=-1, keepdims=True) + EPS) * g


def _pair_tail_columns(wg_ref, wu_ref, tail_ref):
  @pl.when(pl.program_id(0) == 0)
  def _():
    tail_ref[:, :D_FF - FFN_MAIN] = wg_ref[:, FFN_MAIN:]
    tail_ref[:, D_FF - FFN_MAIN:] = wu_ref[:, FFN_MAIN:]


def _swiglu_residual(x, g_ref, wg_ref, wu_ref, wd_ref, tail_ref):
  h = _rms(x, g_ref[...]).astype(BF16)
  acc = jnp.zeros_like(x)
  for c0, c1 in FFN_CHUNKS:
    a = jnp.dot(h, wg_ref[:, c0:c1], preferred_element_type=F32)
    u = jnp.dot(h, wu_ref[:, c0:c1], preferred_element_type=F32)
    s = (a * jax.nn.sigmoid(a) * u).astype(BF16)
    if c1 == FFN_MAIN:
      t = jnp.dot(h, tail_ref[...], preferred_element_type=F32)
      a, u = t[:, :D_FF - FFN_MAIN], t[:, D_FF - FFN_MAIN:]
      s = jnp.concatenate([s, (a * jax.nn.sigmoid(a) * u).astype(BF16)], axis=1)
      c1 = D_FF
    acc = acc + jnp.dot(s, wd_ref[c0:c1, :], preferred_element_type=F32)
  return x + 0.5 * acc


def _head_norm(x, gain, lane_lo):
  sq = x * x
  s_lo = jnp.sum(jnp.where(lane_lo, sq, 0.0), axis=-1, keepdims=True)
  s_all = jnp.sum(sq, axis=-1, keepdims=True)
  ms = jnp.where(lane_lo, s_lo, s_all - s_lo) * (1.0 / HEAD_DIM)
  return x * lax.rsqrt(ms + EPS) * gain


def _ffn_proj_kernel(x_ref, g1_ref, wg_ref, wu_ref, wd_ref, gm_ref, win_ref, qg_ref, kg_ref,
                     x1_ref, hy_ref, a0_ref, a1_ref, a2_ref, perm_ref, tail_ref):
  _pair_tail_columns(wg_ref, wu_ref, tail_ref)
  lane_lo = lax.broadcasted_iota(jnp.int32, (1, LANES), 1) < HEAD_DIM
  kind_gain = (qg_ref[...] * (HEAD_DIM ** -0.5), kg_ref[...], None)
  group_out = (a0_ref, a1_ref, a2_ref)
  for part, rows in enumerate(_row_parts(x_ref.shape[0])):
    x1 = _swiglu_residual(x_ref[rows, :], g1_ref, wg_ref, wu_ref, wd_ref, tail_ref)
    x1_ref[rows, :] = x1
    h = _rms(x1, gm_ref[...]).astype(BF16)
    n_rows = x1.shape[0]
    for c0 in range(0, 3 * HY_WIDTH, ATT_WIDTH):
      hy_ref[rows, c0:c0 + ATT_WIDTH] = jnp.dot(
          h, win_ref[:, c0:c0 + ATT_WIDTH], preferred_element_type=F32).astype(BF16)
    for g, d in enumerate(DILATIONS):
      slabs = []
      for kind, gain in enumerate(kind_gain):
        c0 = 3 * HY_WIDTH + kind * ATT_WIDTH + g * ATT_OUT
        y = jnp.dot(h, win_ref[:, c0:c0 + ATT_OUT], preferred_element_type=F32)
        for s0 in range(0, ATT_OUT, LANES):
          ys = y[:, s0:s0 + LANES]
          slabs.append(ys if gain is None else _head_norm(ys, gain, lane_lo))
      if d == 1:
        a0_ref[rows, :] = jnp.concatenate(slabs, axis=1).astype(BF16)
        continue
      buf = 2 * (g - 1) + part
      for sl, ys in enumerate(slabs):
        perm_ref[buf, sl] = ys
      per_res = n_rows // d
      for r in range(d):
        piece = jnp.concatenate(
            [perm_ref[buf, sl, pl.ds(r, per_res, stride=d), :] for sl in range(len(slabs))], axis=1)
        group_out[g][0, r, pl.ds(part * per_res, per_res), :] = piece.astype(BF16)


def _ffn_proj(x, g1, wg, wu, wd, gm, win, qg, kg, batch):
  t = x.shape[0]
  tm = TOKEN_TILE
  tiles = SEQ // tm
  tok = lambda w: pl.BlockSpec((tm, w), lambda i: (i, 0))
  res = lambda d: pl.BlockSpec((1, d, tm // d, ATT_WIDTH), lambda i: (i // tiles, 0, i % tiles, 0))
  res_shape = lambda d: jax.ShapeDtypeStruct((batch, d, SEQ // d, ATT_WIDTH), BF16)
  d1, d2 = DILATIONS[1], DILATIONS[2]
  return pl.pallas_call(
      _ffn_proj_kernel,
      grid=(t // tm,),
      in_specs=[tok(D_MODEL), _const_spec((1, D_MODEL)), _const_spec((D_MODEL, D_FF)),
                _const_spec((D_MODEL, D_FF)), _const_spec((D_FF, D_MODEL)),
                _const_spec((1, D_MODEL)), _const_spec((D_MODEL, IN_WIDTH)),
                _const_spec((1, LANES)), _const_spec((1, LANES))],
      out_specs=[tok(D_MODEL), tok(3 * HY_WIDTH), tok(ATT_WIDTH), res(d1), res(d2)],
      out_shape=[jax.ShapeDtypeStruct((t, D_MODEL), F32),
                 jax.ShapeDtypeStruct((t, 3 * HY_WIDTH), BF16),
                 jax.ShapeDtypeStruct((t, ATT_WIDTH), BF16), res_shape(d1), res_shape(d2)],
      scratch_shapes=[pltpu.VMEM((2 * TOKEN_PARTS, ATT_WIDTH // LANES, tm // TOKEN_PARTS, LANES),
                                 F32),
                      pltpu.VMEM((D_MODEL, 2 * (D_FF - FFN_MAIN)), BF16)],
      compiler_params=pltpu.CompilerParams(
          dimension_semantics=("arbitrary",), vmem_limit_bytes=VMEM_LIMIT_BYTES),
      name="ffn_proj",
  )(x, g1, wg, wu, wd, gm, win, qg, kg)


def _merge_ffn_kernel(x_ref, yhy_ref, yat_ref, gm_ref, wgate_ref, bgate_ref, why_ref, wat_ref,
                      wout_ref, g2_ref, wg_ref, wu_ref, wd_ref, o_ref, tail_ref):
  _pair_tail_columns(wg_ref, wu_ref, tail_ref)
  for rows in _row_parts(x_ref.shape[0]):
    x = x_ref[rows, :]
    h = _rms(x, gm_ref[...]).astype(BF16)
    gates = jax.nn.sigmoid(
        jnp.dot(h, wgate_ref[...], preferred_element_type=F32) + bgate_ref[...])
    a = jnp.dot(yhy_ref[rows, :], why_ref[...], preferred_element_type=F32)
    b = jnp.dot(yat_ref[rows, :], wat_ref[...], preferred_element_type=F32)
    y = gates[:, :D_MODEL] * a + gates[:, D_MODEL:] * b
    x2 = x + jnp.dot(y.astype(BF16), wout_ref[...], preferred_element_type=F32)
    o_ref[rows, :] = _swiglu_residual(x2, g2_ref, wg_ref, wu_ref, wd_ref, tail_ref)


def _merge_ffn(x, yhy, yat, gm, wgate, bgate, why, wat, wout, g2, wg, wu, wd):
  t = x.shape[0]
  tm = TOKEN_TILE
  tok = lambda w: pl.BlockSpec((tm, w), lambda i: (i, 0))
  return pl.pallas_call(
      _merge_ffn_kernel,
      grid=(t // tm,),
      in_specs=[tok(D_MODEL), tok(HY_WIDTH), tok(ATT_OUT), _const_spec((1, D_MODEL)),
                _const_spec((D_MODEL, 2 * D_MODEL)), _const_spec((1, 2 * D_MODEL)),
                _const_spec((HY_WIDTH, D_MODEL)), _const_spec((ATT_OUT, D_MODEL)),
                _const_spec((D_MODEL, D_MODEL)), _const_spec((1, D_MODEL)),
                _const_spec((D_MODEL, D_FF)), _const_spec((D_MODEL, D_FF)),
                _const_spec((D_FF, D_MODEL))],
      out_specs=tok(D_MODEL),
      out_shape=jax.ShapeDtypeStruct((t, D_MODEL), F32),
      scratch_shapes=[pltpu.VMEM((D_MODEL, 2 * (D_FF - FFN_MAIN)), BF16)],
      compiler_params=pltpu.CompilerParams(
          dimension_semantics=("arbitrary",), vmem_limit_bytes=VMEM_LIMIT_BYTES),
      name="merge_ffn",
  )(x, yhy, yat, gm, wgate, bgate, why, wat, wout, g2, wg, wu, wd)


@functools.lru_cache(maxsize=None)
def _dft_constants_host():
  f1 = np.arange(DFT_F1)
  n1 = np.arange(DFT_N1)
  f2 = np.arange(DFT_F2)
  n2 = np.arange(DFT_N2)
  eye = np.eye(SUBLANES)
  a_ang = 2.0 * np.pi * (((2 * f1 + 1)[:, None] * n1[None, :]) % DFT_N2) / DFT_N2
  ar, ai = np.cos(a_ang), -np.sin(a_ang)
  mk = np.stack([np.einsum("fn,st->fsnt", ar, eye), np.einsum("fn,st->fsnt", ai, eye)], axis=1)
  mk = mk.reshape(UROWS, KROWS)
  scale = 2.0 / DFT_N
  mki = np.stack([np.einsum("fn,st->ntfs", ar * scale, eye),
                  np.einsum("fn,st->ntfs", ai * scale, eye)], axis=3)
  mki = mki.reshape(KROWS, UROWS)
  ph = (n2[None, None, :] * (DFT_N2 * f2[None, :, None] + 2 * f1[:, None, None] + 1)) % (2 * DFT_N)
  d_ang = 2.0 * np.pi * ph / (2 * DFT_N)
  dr = np.cos(d_ang).reshape(DFT_F1, DFT_F2, DFT_GROUPS, SUBLANES)
  di = -np.sin(d_ang).reshape(DFT_F1, DFT_F2, DFT_GROUPS, SUBLANES)
  top = np.stack([dr, -di], axis=3)
  bot = np.stack([di, dr], axis=3)
  md = np.stack([top, bot], axis=1).reshape(DFT_F1, 2 * DFT_F2, 2 * DFT_N2)
  drt = dr.transpose(0, 2, 3, 1)
  dit = di.transpose(0, 2, 3, 1)
  re_rows = np.stack([drt, dit], axis=3)
  im_rows = np.stack([-dit, drt], axis=3)
  mdi = np.stack([re_rows, im_rows], axis=2).reshape(DFT_F1, 2 * DFT_N2, 2 * DFT_F2)
  return tuple(np.asarray(m, np.float32) for m in (mk, mki, md, mdi))


def _dft_constants():
  return tuple(jnp.asarray(m, dtype=BF16) for m in _dft_constants_host())


def _grouped_loop(count, group, body):
  def outer(i, carry):
    base = i * group
    for k in range(group):
      body(base + k, base, k)
    return carry

  lax.fori_loop(0, count // group, outer, 0)


def _tile_rows(base, scale, offset, size):
  return pl.ds(pl.multiple_of(base * scale, size) + offset, size)


def _stage_k_forward(z_ref, mk_ref, u_ref):
  def body(j, base, k):
    tiles = [z_ref[_tile_rows(base, SUBLANES, DFT_N2 * n1 + SUBLANES * k, SUBLANES), :]
             for n1 in range(DFT_N1)]
    rhs = jnp.concatenate(tiles, axis=0).astype(BF16)
    u_ref[j] = jnp.dot(mk_ref[...], rhs, preferred_element_type=F32).astype(BF16)

  _grouped_loop(DFT_GROUPS, K_UNROLL, body)


def _gather_f1(u_ref, base, k):
  rows = _tile_rows(base, BF16_ROWS, BF16_ROWS * k, BF16_ROWS)
  return jnp.concatenate([u_ref[j, rows, :] for j in range(DFT_GROUPS)], axis=0)


def _scatter_f1(u_ref, base, k, w):
  rows = _tile_rows(base, BF16_ROWS, BF16_ROWS * k, BF16_ROWS)
  wb = w.astype(BF16)
  for j in range(DFT_GROUPS):
    u_ref[j, rows, :] = wb[BF16_ROWS * j:BF16_ROWS * (j + 1), :]


def _spectrum_slot(base, k):
  per_group = UROWS // (2 * DFT_F2)
  return (base // per_group + k // per_group, pl.ds((k % per_group) * 2 * DFT_F2, 2 * DFT_F2),
          slice(None))


def _stage_k_inverse(u_ref, mki_ref, emit):
  def body(j, base, k):
    y = jnp.dot(mki_ref[...], u_ref[j], preferred_element_type=F32)
    for n1 in range(DFT_N1):
      rows = _tile_rows(base, SUBLANES, DFT_N2 * n1 + SUBLANES * k, SUBLANES)
      emit(rows, y[SUBLANES * n1:SUBLANES * (n1 + 1), :])

  _grouped_loop(DFT_GROUPS, K_UNROLL, body)


def _hy_filter_kernel(z_ref, w1_ref, b1_ref, w2_ref, b2_ref, w3f_ref, w3b_ref, dl_ref,
                      mk_ref, md_ref, h_ref, hid_ref, k_ref, u_ref):
  hp = lax.Precision.HIGHEST

  @pl.when((pl.program_id(0) == 0) & (pl.program_id(1) == 0))
  def _():
    hid = jnp.sin(
        jnp.dot(z_ref[...], w1_ref[...], precision=hp, preferred_element_type=F32) + b1_ref[...])
    hid_ref[...] = jnp.sin(
        jnp.dot(hid, w2_ref[...], precision=hp, preferred_element_type=F32) + b2_ref[...])

  hid = hid_ref[...]
  decay = jnp.exp(-z_ref[:, 0:1] * dl_ref[...])
  kf = jnp.dot(hid, w3f_ref[...], precision=hp, preferred_element_type=F32) * decay
  kb = jnp.dot(hid, w3b_ref[...], precision=hp, preferred_element_type=F32) * decay
  row = lax.broadcasted_iota(jnp.int32, kb.shape, 0)
  kb = jnp.where(row == 0, 0.0, kb)
  rs = lax.rsqrt(jnp.sum(kf * kf + kb * kb, axis=0, keepdims=True) + EPS)
  c = kf.shape[1]
  k_ref[:, :c] = kf
  k_ref[:, c:] = kb
  _stage_k_forward(k_ref, mk_ref, u_ref)

  def body(f1, base, k):
    x = jnp.dot(md_ref[f1], _gather_f1(u_ref, base, k), preferred_element_type=F32)
    h_ref[0, f1, pl.ds(0, DFT_F2), :] = (x[:DFT_F2, :c] + x[:DFT_F2, c:]) * rs
    h_ref[0, f1, pl.ds(DFT_F2, DFT_F2), :] = (x[DFT_F2:, :c] - x[DFT_F2:, c:]) * rs

  _grouped_loop(DFT_F1, D_UNROLL, body)


def _hy_filter(zfeat, w1, b1, w2, b2, w3, deltas, mk, md):
  c = HY_CB
  ncb = HY_WIDTH // c
  return pl.pallas_call(
      _hy_filter_kernel,
      grid=(HY_ORDER, ncb),
      in_specs=[_const_spec(zfeat.shape), _const_spec(w1.shape), _const_spec(b1.shape),
                _const_spec(w2.shape), _const_spec(b2.shape),
                pl.BlockSpec((HY_FILT_HIDDEN, c), lambda o, cb: (0, o * 2 * ncb + cb)),
                pl.BlockSpec((HY_FILT_HIDDEN, c), lambda o, cb: (0, o * 2 * ncb + ncb + cb)),
                pl.BlockSpec((1, c), lambda o, cb: (0, cb)),
                _const_spec(mk.shape), _const_spec(md.shape)],
      out_specs=pl.BlockSpec((1, DFT_F1, 2 * DFT_F2, c), lambda o, cb: (o, 0, 0, cb)),
      out_shape=jax.ShapeDtypeStruct((HY_ORDER, DFT_F1, 2 * DFT_F2, HY_WIDTH), F32),
      scratch_shapes=[pltpu.VMEM((SEQ, HY_FILT_HIDDEN), F32),
                      pltpu.VMEM((SEQ, 2 * c), F32),
                      pltpu.VMEM((DFT_GROUPS, UROWS, 2 * c), BF16)],
      compiler_params=pltpu.CompilerParams(
          dimension_semantics=("arbitrary", "arbitrary"), vmem_limit_bytes=VMEM_LIMIT_BYTES),
      name="hy_filter",
  )(zfeat, w1, b1, w2, b2, w3, w3, deltas, mk, md)


def _conv3(p_ref, half, cw_ref, k):
  u = p_ref[pl.ds(half * SEQ, SEQ), :].astype(F32)
  row = lax.broadcasted_iota(jnp.int32, u.shape, 0)
  prev = jnp.where(row == 0, 0.0, pltpu.roll(u, 1, 0))
  nxt = jnp.where(row == SEQ - 1, 0.0, pltpu.roll(u, SEQ - 1, 0))
  w = cw_ref[k]
  return prev * w[0:1] + u * w[1:2] + nxt * w[2:3] + w[3:4]


def _hyena_kernel(pv_ref, pa_ref, pb_ref, cw_ref, skip_ref, h_ref, mk_ref, mki_ref, md_ref,
                  mdi_ref, o_ref, z_ref, y_ref, u_ref, w_ref):
  c = HY_CB
  for half in range(2):
    z_ref[:, half * c:(half + 1) * c] = _conv3(pv_ref, half, cw_ref, 0)
  gates = (pa_ref, pb_ref)
  for order in range(HY_ORDER):
    _stage_k_forward(z_ref, mk_ref, u_ref)

    def spectral(f1, base, k, order=order):
      x = jnp.dot(md_ref[f1], _gather_f1(u_ref, base, k), preferred_element_type=F32)
      xr, xi = x[:DFT_F2], x[DFT_F2:]
      hr = h_ref[order, f1, pl.ds(0, DFT_F2), :]
      hi = h_ref[order, f1, pl.ds(DFT_F2, DFT_F2), :]
      hr = jnp.concatenate([hr, hr], axis=1)
      hi = jnp.concatenate([hi, hi], axis=1)
      y = jnp.concatenate([xr * hr - xi * hi, xr * hi + xi * hr], axis=0).astype(BF16)
      w_ref[_spectrum_slot(base, k)] = y

    _grouped_loop(DFT_F1, D_UNROLL, spectral)

    def inverse_d(f1, base, k):
      w = jnp.dot(mdi_ref[f1], w_ref[_spectrum_slot(base, k)], preferred_element_type=F32)
      _scatter_f1(u_ref, base, k, w)

    _grouped_loop(DFT_F1, D_UNROLL, inverse_d)

    def emit(rows, tile):
      y_ref[rows, :] = tile

    _stage_k_inverse(u_ref, mki_ref, emit)
    for half in range(2):
      lanes = slice(half * c, (half + 1) * c)
      gate = _conv3(gates[order], half, cw_ref, 1 + order)
      z = gate * (y_ref[:, lanes] + skip_ref[order:order + 1, :] * z_ref[:, lanes])
      if order + 1 < HY_ORDER:
        z_ref[:, lanes] = z
      else:
        o_ref[pl.ds(half * SEQ, SEQ), :] = z.astype(BF16)


def _hyena(p, cw, skip, hspec, mk, mki, md, mdi, batch):
  c = HY_CB
  ncb = HY_WIDTH // c
  col = lambda k: pl.BlockSpec((2 * SEQ, c), lambda cb, bp: (bp, k * ncb + cb))
  return pl.pallas_call(
      _hyena_kernel,
      grid=(ncb, batch // 2),
      in_specs=[col(0), col(1), col(2),
                pl.BlockSpec((3, 4, c), lambda cb, bp: (0, 0, cb)),
                pl.BlockSpec((HY_ORDER, c), lambda cb, bp: (0, cb)),
                pl.BlockSpec((HY_ORDER, DFT_F1, 2 * DFT_F2, c), lambda cb, bp: (0, 0, 0, cb),
                             pipeline_mode=pl.Buffered(1)),
                _const_spec(mk.shape), _const_spec(mki.shape), _const_spec(md.shape),
                _const_spec(mdi.shape)],
      out_specs=pl.BlockSpec((2 * SEQ, c), lambda cb, bp: (bp, cb)),
      out_shape=jax.ShapeDtypeStruct((batch * SEQ, HY_WIDTH), BF16),
      scratch_shapes=[pltpu.VMEM((SEQ, 2 * c), F32), pltpu.VMEM((SEQ, 2 * c), F32),
                      pltpu.VMEM((DFT_GROUPS, UROWS, 2 * c), BF16),
                      pltpu.VMEM((DFT_GROUPS, UROWS, 2 * c), BF16)],
      compiler_params=pltpu.CompilerParams(
          dimension_semantics=("arbitrary", "arbitrary"), vmem_limit_bytes=VMEM_LIMIT_BYTES),
      name="hyena",
  )(p, p, p, cw, skip, hspec, mk, mki, md, mdi)


def _t5_bucket(rel):
  half = REL_BUCKETS // 2
  exact = half // 2
  ret = jnp.where(rel > 0, half, 0)
  n = jnp.abs(rel)
  nf = jnp.maximum(n, 1).astype(F32)
  large = exact + (jnp.log(nf / exact) / math.log(REL_MAX_DISTANCE / exact) * (half - exact)).astype(jnp.int32)
  large = jnp.minimum(large, half - 1)
  return ret + jnp.where(n < exact, n, large)


def _bucket_maps():
  qi = jnp.arange(ATT_BQ)[:, None]
  ki = jnp.arange(ATT_BK)[None, :]
  rel = ki - N_SIDE - qi
  maps = [jnp.where(jnp.abs(rel) <= N_SIDE, _t5_bucket(rel * d), -1) for d in DILATIONS]
  return jnp.stack(maps).astype(jnp.int32)


def _dil_attn_kernel(tab_ref, *refs):
  qkv_refs = refs[:9]
  bkt_ref, o_ref = refs[9:11]
  ks_ref, vs_ref, acc_ref, m_ref, l_ref, bias_ref = refs[11:]
  pair = pl.program_id(1)
  lane_lo = lax.broadcasted_iota(jnp.int32, (1, LANES), 1) < HEAD_DIM
  key_idx = lax.broadcasted_iota(jnp.int32, (1, ATT_BK), 1)

  @pl.when((pl.program_id(0) == 0) & (pair == 0))
  def _():
    for g in range(N_GROUPS):
      bkt = bkt_ref[g]
      outside = jnp.where(bkt < 0, NEG, 0.0)
      for c in range(HEADS_PER_GROUP):
        b = outside
        for t in range(REL_BUCKETS):
          b = b + jnp.where(bkt == t, tab_ref[t, g * HEADS_PER_GROUP + c], 0.0)
        bias_ref[g * HEADS_PER_GROUP + c] = b

  group_order = sorted(range(N_GROUPS), key=lambda g: -DILATIONS[g])
  for g in group_order:
    d = DILATIONS[g]
    q_ref, k_ref, v_ref = qkv_refs[3 * g:3 * g + 3]
    m_sub = SEQ // d
    span = m_sub + 2 * N_SIDE
    blocks = m_sub // ATT_BQ

    def residue_rows(ref, r, start, size, d=d):
      if d == 1:
        return ref[pl.ds(start, size), :]
      return ref[0, r, pl.ds(start, size), :]

    zeros = jnp.zeros((N_SIDE, LANES), BF16)
    for r in range(d):
      for src, dst in ((k_ref, ks_ref), (v_ref, vs_ref)):
        dst[pl.ds(r * span, N_SIDE), :] = zeros
        dst[pl.ds(r * span + N_SIDE, m_sub), :] = residue_rows(src, r, 0, m_sub)
        dst[pl.ds(r * span + N_SIDE + m_sub, N_SIDE), :] = zeros

    def block(t, carry, d=d, m_sub=m_sub, g=g, span=span, blocks=blocks,
              residue_rows=residue_rows, q_ref=q_ref):
      r = t // blocks
      i = t % blocks
      q = residue_rows(q_ref, r, pl.multiple_of(i * ATT_BQ, ATT_BQ), ATT_BQ)
      key_rows = pl.ds(pl.multiple_of(r * span + i * ATT_BQ, ATT_BQ), ATT_BK)
      k = ks_ref[key_rows, :]
      v = vs_ref[key_rows, :]
      start = r + d * ATT_BQ * i
      rows_q = pl.ds(start, ATT_BQ) if d == 1 else pl.ds(start, ATT_BQ, stride=d)
      kpos = ATT_BQ * i - N_SIDE + key_idx
      kvalid = (kpos >= 0) & (kpos < m_sub)
      zero = jnp.zeros_like(q)
      outs, maxs, sums = [], [], []
      for h in range(2):
        qh = jnp.where(lane_lo if h == 0 else jnp.logical_not(lane_lo), q, zero)
        s = lax.dot_general(qh, k, (((1,), (1,)), ((), ())), preferred_element_type=F32)
        s = jnp.where(kvalid, s + bias_ref[g * HEADS_PER_GROUP + 2 * pair + h], NEG)
        mx = jnp.max(s, axis=-1, keepdims=True)
        p = jnp.exp(s - mx)
        sums.append(jnp.sum(p, axis=-1, keepdims=True))
        maxs.append(mx)
        outs.append(jnp.dot(p.astype(BF16), v, preferred_element_type=F32))
      o = jnp.where(lane_lo, outs[0], outs[1])
      mx = jnp.where(lane_lo, maxs[0], maxs[1])
      sm = jnp.where(lane_lo, sums[0], sums[1])
      if g == group_order[0]:
        acc_ref[rows_q, :] = o
        m_ref[rows_q, :] = mx
        l_ref[rows_q, :] = sm
      else:
        m_old = m_ref[rows_q, :]
        m_new = jnp.maximum(m_old, mx)
        a_old = jnp.exp(m_old - m_new)
        a_new = jnp.exp(mx - m_new)
        acc_ref[rows_q, :] = acc_ref[rows_q, :] * a_old + o * a_new
        l_ref[rows_q, :] = l_ref[rows_q, :] * a_old + sm * a_new
        m_ref[rows_q, :] = m_new
      return carry

    lax.fori_loop(0, SEQ // ATT_BQ, block, 0, unroll=ATT_UNROLL)

  o_ref[...] = (acc_ref[...] / l_ref[...]).astype(BF16)


def _dil_attn(groups, rel_bias, bkt, batch):
  per_group = ATT_OUT // LANES

  def col(g, kind):
    d = DILATIONS[g]
    if d == 1:
      return pl.BlockSpec((SEQ, LANES), lambda b, pr, tab: (b, kind * per_group + pr))
    return pl.BlockSpec((1, d, SEQ // d, LANES),
                        lambda b, pr, tab: (b, 0, 0, kind * per_group + pr))

  in_specs = [col(g, kind) for g in range(N_GROUPS) for kind in range(3)]
  in_specs += [pl.BlockSpec(bkt.shape, lambda b, pr, tab: (0, 0, 0))]
  key_rows = SEQ + 2 * N_SIDE * max(DILATIONS)
  grid_spec = pltpu.PrefetchScalarGridSpec(
      num_scalar_prefetch=1,
      grid=(batch, per_group),
      in_specs=in_specs,
      out_specs=pl.BlockSpec((SEQ, LANES), lambda b, pr, tab: (b, pr)),
      scratch_shapes=[pltpu.VMEM((key_rows, LANES), BF16), pltpu.VMEM((key_rows, LANES), BF16),
                      pltpu.VMEM((SEQ, LANES), F32), pltpu.VMEM((SEQ, LANES), F32),
                      pltpu.VMEM((SEQ, LANES), F32),
                      pltpu.VMEM((N_GROUPS * HEADS_PER_GROUP, ATT_BQ, ATT_BK), F32)])
  operands = [groups[g] for g in range(N_GROUPS) for _ in range(3)]
  return pl.pallas_call(
      _dil_attn_kernel,
      grid_spec=grid_spec,
      out_shape=jax.ShapeDtypeStruct((batch * SEQ, ATT_OUT), BF16),
      compiler_params=pltpu.CompilerParams(
          dimension_semantics=("arbitrary", "arbitrary"), vmem_limit_bytes=VMEM_LIMIT_BYTES),
      name="dil_attn",
  )(rel_bias, *operands, bkt)


def _filter_features():
  t = jnp.linspace(0.0, 1.0, SEQ, dtype=F32)[:, None]
  w = (2.0 * math.pi / SEQ) * jnp.arange(SEQ, dtype=F32)[:, None]
  f = jnp.linspace(1e-4, HY_POS_BANDS - 1, HY_POS_BANDS, dtype=F32)[None]
  z = jnp.concatenate([t, jnp.cos(f * w), -jnp.sin(f * w)], axis=-1)
  return jnp.pad(z, ((0, 0), (0, LANES - HY_EMB)))


def _decay_rates():
  max_decay = math.log(HY_DECAY_TARGET) / HY_FAST_DECAY_PCT
  min_decay = math.log(HY_DECAY_TARGET) / HY_SLOW_DECAY_PCT
  return jnp.abs(jnp.linspace(min_decay, max_decay, HY_WIDTH, dtype=F32))[None]


def kernel(x, rel_bias, ffn1_norm, ffn1_w_gate, ffn1_w_up, ffn1_w_down, mix_norm, w_in, w_gate, b_gate, hy_conv_w, hy_conv_b, hy_filt_w1, hy_filt_b1, hy_filt_w2, hy_filt_b2, hy_filt_w3, hy_skip, q_norm, k_norm, w_hy_proj, w_at_proj, w_out, ffn2_norm, ffn2_w_gate, ffn2_w_up, ffn2_w_down):
  batch, seq, d_model = x.shape
  assert (seq, d_model) == (SEQ, D_MODEL) and batch % 2 == 0
  depth = ffn1_norm.shape[0]
  mk, mki, md, mdi = _dft_constants()
  zfeat = _filter_features()
  deltas = _decay_rates()
  bkt = _bucket_maps()
  bf = lambda w: w.astype(BF16)
  row = lambda v: v.reshape(1, -1)

  xt = x.reshape(batch * seq, d_model)
  for l in range(depth):
    xt, p, *groups = _ffn_proj(
        xt, row(ffn1_norm[l]), bf(ffn1_w_gate[l]), bf(ffn1_w_up[l]), bf(ffn1_w_down[l]),
        row(mix_norm[l]), bf(w_in[l]), row(jnp.tile(q_norm[l], 2)),
        row(jnp.tile(k_norm[l], 2)), batch)
    w1 = jnp.pad(hy_filt_w1[l], ((0, LANES - HY_EMB), (0, 0)))
    hspec = _hy_filter(zfeat, w1, row(hy_filt_b1[l]), hy_filt_w2[l], row(hy_filt_b2[l]),
                       hy_filt_w3[l], deltas, mk, md)
    cw = jnp.concatenate([hy_conv_w[l], hy_conv_b[l][None]], axis=0)
    cw = cw.reshape(4, 3, HY_WIDTH).transpose(1, 0, 2)
    yhy = _hyena(p, cw, hy_skip[l], hspec, mk, mki, md, mdi, batch)
    yat = _dil_attn(groups, rel_bias, bkt, batch)
    xt = _merge_ffn(xt, yhy, yat, row(mix_norm[l]), bf(w_gate[l]), row(b_gate[l]),
                    bf(w_hy_proj[l]), bf(w_at_proj[l]), bf(w_out[l]), row(ffn2_norm[l]),
                    bf(ffn2_w_gate[l]), bf(ffn2_w_up[l]), bf(ffn2_w_down[l]))
  return xt.reshape(batch, seq, d_model)
```

```python
import functools
import math

import jax
import jax.numpy as jnp
import numpy as np
from jax import lax
from jax.experimental import pallas as pl
from jax.experimental.pallas import tpu as pltpu

F32 = jnp.float32
BF16 = jnp.bfloat16

D_MODEL = 1024
SEQ = 4096
HEAD_DIM = 64
HY_WIDTH = 512
HY_ORDER = 2
HY_FILT_HIDDEN = 64
HY_POS_BANDS = 16
HY_EMB = 1 + 2 * HY_POS_BANDS
HY_FAST_DECAY_PCT = 0.3
HY_SLOW_DECAY_PCT = 1.5
HY_DECAY_TARGET = 1e-2
WINDOWS = (128, 512, 2048)
DILATIONS = (1, 4, 16)
N_GROUPS = 3
HEADS_PER_GROUP = 4
ATT_WIDTH = N_GROUPS * HEADS_PER_GROUP * HEAD_DIM
ATT_OUT = HEADS_PER_GROUP * HEAD_DIM
REL_BUCKETS = 32
REL_MAX_DISTANCE = 1024
D_FF = (8 * D_MODEL // 3) // 128 * 128
IN_WIDTH = 3 * HY_WIDTH + 3 * ATT_WIDTH
EPS = 1e-6
NEG = -1e30

SUBLANES = 8
LANES = 128
BF16_ROWS = 16
VMEM_LIMIT_BYTES = 60 * 1024 * 1024

DFT_N = 2 * SEQ
DFT_N2 = 128
DFT_N1 = SEQ // DFT_N2
DFT_F1 = 64
DFT_F2 = SEQ // DFT_F1
DFT_GROUPS = DFT_N2 // SUBLANES
KROWS = DFT_N1 * SUBLANES
UROWS = 2 * DFT_F1 * SUBLANES

N_SIDE = 64
ATT_BQ = 128
ATT_BK = ATT_BQ + 2 * N_SIDE

FFN_MAIN = D_FF // 256 * 256
FFN_CHUNKS = ((0, FFN_MAIN // 2), (FFN_MAIN // 2, FFN_MAIN))
TOKEN_TILE = 512
TOKEN_PARTS = 2
MERGE_TILE = 1024
HY_CB = 128
K_UNROLL = 8
D_UNROLL = 16
ATT_UNROLL = 8


def _const_spec(shape):
  nd = len(shape)
  return pl.BlockSpec(shape, lambda *_: (0,) * nd, pipeline_mode=pl.Buffered(1))


def _row_parts(rows):
  part = TOKEN_TILE // TOKEN_PARTS
  return [pl.ds(i * part, part) for i in range(rows // part)]


def _rms(x, g):
  return x * lax.rsqrt(jnp.mean(x * x, axis=-1, keepdims=True) + EPS) * g


def _pair_tail_columns(wg_ref, wu_ref, tail_ref):
  @pl.when(pl.program_id(0) == 0)
  def _():
    tail_ref[:, :D_FF - FFN_MAIN] = wg_ref[:, FFN_MAIN:]
    tail_ref[:, D_FF - FFN_MAIN:] = wu_ref[:, FFN_MAIN:]


def _swiglu_residual(x, g_ref, wg_ref, wu_ref, wd_ref, tail_ref):
  h = _rms(x, g_ref[...]).astype(BF16)
  acc = jnp.zeros_like(x)
  for c0, c1 in FFN_CHUNKS:
    a = jnp.dot(h, wg_ref[:, c0:c1], preferred_element_type=F32)
    u = jnp.dot(h, wu_ref[:, c0:c1], preferred_element_type=F32)
    s = (a * jax.nn.sigmoid(a) * u).astype(BF16)
    if c1 == FFN_MAIN:
      t = jnp.dot(h, tail_ref[...], preferred_element_type=F32)
      a, u = t[:, :D_FF - FFN_MAIN], t[:, D_FF - FFN_MAIN:]
      s = jnp.concatenate([s, (a * jax.nn.sigmoid(a) * u).astype(BF16)], axis=1)
      c1 = D_FF
    acc = acc + jnp.dot(s, wd_ref[c0:c1, :], preferred_element_type=F32)
  return x + 0.5 * acc


def _head_norm(x, gain, lane_lo):
  sq = x * x
  s_lo = jnp.sum(jnp.where(lane_lo, sq, 0.0), axis=-1, keepdims=True)
  s_all = jnp.sum(sq, axis=-1, keepdims=True)
  ms = jnp.where(lane_lo, s_lo, s_all - s_lo) * (1.0 / HEAD_DIM)
  return x * lax.rsqrt(ms + EPS) * gain


def _ffn_proj_kernel(x_ref, g1_ref, wg_ref, wu_ref, wd_ref, gm_ref, win_ref, qg_ref, kg_ref,
                     x1_ref, hy_ref, a0_ref, a1_ref, a2_ref, perm_ref, tail_ref):
  _pair_tail_columns(wg_ref, wu_ref, tail_ref)
  lane_lo = lax.broadcasted_iota(jnp.int32, (1, LANES), 1) < HEAD_DIM
  kind_gain = (qg_ref[...] * (HEAD_DIM ** -0.5), kg_ref[...], None)
  group_out = (a0_ref, a1_ref, a2_ref)
  for part, rows in enumerate(_row_parts(x_ref.shape[0])):
    x1 = _swiglu_residual(x_ref[rows, :], g1_ref, wg_ref, wu_ref, wd_ref, tail_ref)
    x1_ref[rows, :] = x1
    h = _rms(x1, gm_ref[...]).astype(BF16)
    n_rows = x1.shape[0]
    for c0 in range(0, 3 * HY_WIDTH, ATT_WIDTH):
      hy_ref[rows, c0:c0 + ATT_WIDTH] = jnp.dot(
          h, win_ref[:, c0:c0 + ATT_WIDTH], preferred_element_type=F32).astype(BF16)
    n_slabs = ATT_WIDTH // LANES
    for g, d in enumerate(DILATIONS):
      buf = 2 * (g - 1) + part
      for kind, gain in enumerate(kind_gain):
        c0 = 3 * HY_WIDTH + kind * ATT_WIDTH + g * ATT_OUT
        y = jnp.dot(h, win_ref[:, c0:c0 + ATT_OUT], preferred_element_type=F32)
        for s0 in range(0, ATT_OUT, LANES):
          ys = y[:, s0:s0 + LANES]
          ys = ys if gain is None else _head_norm(ys, gain, lane_lo)
          if d == 1:
            a0_ref[rows, kind * ATT_OUT + s0:kind * ATT_OUT + s0 + LANES] = ys.astype(BF16)
          else:
            perm_ref[buf, (kind * ATT_OUT + s0) // LANES] = ys
      if d == 1:
        continue
      per_res = n_rows // d
      for r in range(d):
        piece = jnp.concatenate(
            [perm_ref[buf, sl, pl.ds(r, per_res, stride=d), :] for sl in range(n_slabs)], axis=1)
        group_out[g][0, r, pl.ds(part * per_res, per_res), :] = piece.astype(BF16)


def _ffn_proj(x, g1, wg, wu, wd, gm, win, qg, kg, batch):
  t = x.shape[0]
  tm = TOKEN_TILE
  tiles = SEQ // tm
  tok = lambda w: pl.BlockSpec((tm, w), lambda i: (i, 0))
  res = lambda d: pl.BlockSpec((1, d, tm // d, ATT_WIDTH), lambda i: (i // tiles, 0, i % tiles, 0))
  res_shape = lambda d: jax.ShapeDtypeStruct((batch, d, SEQ // d, ATT_WIDTH), BF16)
  d1, d2 = DILATIONS[1], DILATIONS[2]
  return pl.pallas_call(
      _ffn_proj_kernel,
      grid=(t // tm,),
      in_specs=[tok(D_MODEL), _const_spec((1, D_MODEL)), _const_spec((D_MODEL, D_FF)),
                _const_spec((D_MODEL, D_FF)), _const_spec((D_FF, D_MODEL)),
                _const_spec((1, D_MODEL)), _const_spec((D_MODEL, IN_WIDTH)),
                _const_spec((1, LANES)), _const_spec((1, LANES))],
      out_specs=[tok(D_MODEL), tok(3 * HY_WIDTH), tok(ATT_WIDTH), res(d1), res(d2)],
      out_shape=[jax.ShapeDtypeStruct((t, D_MODEL), F32),
                 jax.ShapeDtypeStruct((t, 3 * HY_WIDTH), BF16),
                 jax.ShapeDtypeStruct((t, ATT_WIDTH), BF16), res_shape(d1), res_shape(d2)],
      scratch_shapes=[pltpu.VMEM((2 * TOKEN_PARTS, ATT_WIDTH // LANES, tm // TOKEN_PARTS, LANES),
                                 F32),
                      pltpu.VMEM((D_MODEL, 2 * (D_FF - FFN_MAIN)), BF16)],
      compiler_params=pltpu.CompilerParams(
          dimension_semantics=("arbitrary",), vmem_limit_bytes=VMEM_LIMIT_BYTES),
      name="ffn_proj",
  )(x, g1, wg, wu, wd, gm, win, qg, kg)


def _merge_ffn_kernel(x_ref, yhy_ref, yat_ref, gm_ref, wgate_ref, bgate_ref, why_ref, wat_ref,
                      wout_ref, g2_ref, wg_ref, wu_ref, wd_ref, o_ref, tail_ref):
  _pair_tail_columns(wg_ref, wu_ref, tail_ref)
  for rows in _row_parts(x_ref.shape[0]):
    x = x_ref[rows, :]
    h = _rms(x, gm_ref[...]).astype(BF16)
    gates = jax.nn.sigmoid(
        jnp.dot(h, wgate_ref[...], preferred_element_type=F32) + bgate_ref[...])
    a = jnp.dot(yhy_ref[rows, :], why_ref[...], preferred_element_type=F32)
    b = jnp.dot(yat_ref[rows, :], wat_ref[...], preferred_element_type=F32)
    y = gates[:, :D_MODEL] * a + gates[:, D_MODEL:] * b
    x2 = x + jnp.dot(y.astype(BF16), wout_ref[...], preferred_element_type=F32)
    o_ref[rows, :] = _swiglu_residual(x2, g2_ref, wg_ref, wu_ref, wd_ref, tail_ref)


def _merge_ffn(x, yhy, yat, gm, wgate, bgate, why, wat, wout, g2, wg, wu, wd):
  t = x.shape[0]
  tm = MERGE_TILE
  tok = lambda w: pl.BlockSpec((tm, w), lambda i: (i, 0))
  return pl.pallas_call(
      _merge_ffn_kernel,
      grid=(t // tm,),
      in_specs=[tok(D_MODEL), tok(HY_WIDTH), tok(ATT_OUT), _const_spec((1, D_MODEL)),
                _const_spec((D_MODEL, 2 * D_MODEL)), _const_spec((1, 2 * D_MODEL)),
                _const_spec((HY_WIDTH, D_MODEL)), _const_spec((ATT_OUT, D_MODEL)),
                _const_spec((D_MODEL, D_MODEL)), _const_spec((1, D_MODEL)),
                _const_spec((D_MODEL, D_FF)), _const_spec((D_MODEL, D_FF)),
                _const_spec((D_FF, D_MODEL))],
      out_specs=tok(D_MODEL),
      out_shape=jax.ShapeDtypeStruct((t, D_MODEL), F32),
      scratch_shapes=[pltpu.VMEM((D_MODEL, 2 * (D_FF - FFN_MAIN)), BF16)],
      compiler_params=pltpu.CompilerParams(
          dimension_semantics=("arbitrary",), vmem_limit_bytes=VMEM_LIMIT_BYTES),
      name="merge_ffn",
  )(x, yhy, yat, gm, wgate, bgate, why, wat, wout, g2, wg, wu, wd)


@functools.lru_cache(maxsize=None)
def _dft_constants_host():
  f1 = np.arange(DFT_F1)
  n1 = np.arange(DFT_N1)
  f2 = np.arange(DFT_F2)
  n2 = np.arange(DFT_N2)
  eye = np.eye(SUBLANES)
  a_ang = 2.0 * np.pi * (((2 * f1 + 1)[:, None] * n1[None, :]) % DFT_N2) / DFT_N2
  ar, ai = np.cos(a_ang), -np.sin(a_ang)
  mk = np.stack([np.einsum("fn,st->fsnt", ar, eye), np.einsum("fn,st->fsnt", ai, eye)], axis=1)
  mk = mk.reshape(UROWS, KROWS)
  scale = 2.0 / DFT_N
  mki = np.stack([np.einsum("fn,st->ntfs", ar * scale, eye),
                  np.einsum("fn,st->ntfs", ai * scale, eye)], axis=3)
  mki = mki.reshape(KROWS, UROWS)
  ph = (n2[None, None, :] * (DFT_N2 * f2[None, :, None] + 2 * f1[:, None, None] + 1)) % (2 * DFT_N)
  d_ang = 2.0 * np.pi * ph / (2 * DFT_N)
  dr = np.cos(d_ang).reshape(DFT_F1, DFT_F2, DFT_GROUPS, SUBLANES)
  di = -np.sin(d_ang).reshape(DFT_F1, DFT_F2, DFT_GROUPS, SUBLANES)
  top = np.stack([dr, -di], axis=3)
  bot = np.stack([di, dr], axis=3)
  md = np.stack([top, bot], axis=1).reshape(DFT_F1, 2 * DFT_F2, 2 * DFT_N2)
  drt = dr.transpose(0, 2, 3, 1)
  dit = di.transpose(0, 2, 3, 1)
  re_rows = np.stack([drt, dit], axis=3)
  im_rows = np.stack([-dit, drt], axis=3)
  mdi = np.stack([re_rows, im_rows], axis=2).reshape(DFT_F1, 2 * DFT_N2, 2 * DFT_F2)
  return tuple(np.asarray(m, np.float32) for m in (mk, mki, md, mdi))


def _dft_constants():
  return tuple(jnp.asarray(m, dtype=BF16) for m in _dft_constants_host())


def _grouped_loop(count, group, body):
  def outer(i, carry):
    base = i * group
    for k in range(group):
      body(base + k, base, k)
    return carry

  lax.fori_loop(0, count // group, outer, 0)


def _tile_rows(base, scale, offset, size):
  return pl.ds(pl.multiple_of(base * scale, size) + offset, size)


def _stage_k_forward(z_ref, mk_ref, u_ref):
  def body(j, base, k):
    tiles = [z_ref[_tile_rows(base, SUBLANES, DFT_N2 * n1 + SUBLANES * k, SUBLANES), :]
             for n1 in range(DFT_N1)]
    rhs = jnp.concatenate(tiles, axis=0).astype(BF16)
    u_ref[j] = jnp.dot(mk_ref[...], rhs, preferred_element_type=F32).astype(BF16)

  _grouped_loop(DFT_GROUPS, K_UNROLL, body)


def _gather_f1(u_ref, base, k):
  rows = _tile_rows(base, BF16_ROWS, BF16_ROWS * k, BF16_ROWS)
  return jnp.concatenate([u_ref[j, rows, :] for j in range(DFT_GROUPS)], axis=0)


def _scatter_f1(u_ref, base, k, w):
  rows = _tile_rows(base, BF16_ROWS, BF16_ROWS * k, BF16_ROWS)
  wb = w.astype(BF16)
  for j in range(DFT_GROUPS):
    u_ref[j, rows, :] = wb[BF16_ROWS * j:BF16_ROWS * (j + 1), :]


def _spectrum_slot(base, k):
  per_group = UROWS // (2 * DFT_F2)
  return (base // per_group + k // per_group, pl.ds((k % per_group) * 2 * DFT_F2, 2 * DFT_F2),
          slice(None))


def _stage_k_inverse(u_ref, mki_ref, emit):
  def body(j, base, k):
    y = jnp.dot(mki_ref[...], u_ref[j], preferred_element_type=F32)
    for n1 in range(DFT_N1):
      rows = _tile_rows(base, SUBLANES, DFT_N2 * n1 + SUBLANES * k, SUBLANES)
      emit(rows, y[SUBLANES * n1:SUBLANES * (n1 + 1), :])

  _grouped_loop(DFT_GROUPS, K_UNROLL, body)


def _hy_filter_kernel(z_ref, w1_ref, b1_ref, w2_ref, b2_ref, w3f_ref, w3b_ref, dl_ref,
                      mk_ref, md_ref, h_ref, hid_ref, k_ref, u_ref):
  hp = lax.Precision.HIGHEST

  @pl.when((pl.program_id(0) == 0) & (pl.program_id(1) == 0))
  def _():
    hid = jnp.sin(
        jnp.dot(z_ref[...], w1_ref[...], precision=hp, preferred_element_type=F32) + b1_ref[...])
    hid_ref[...] = jnp.sin(
        jnp.dot(hid, w2_ref[...], precision=hp, preferred_element_type=F32) + b2_ref[...])

  hid = hid_ref[...]
  decay = jnp.exp(-z_ref[:, 0:1] * dl_ref[...])
  kf = jnp.dot(hid, w3f_ref[...], precision=hp, preferred_element_type=F32) * decay
  kb = jnp.dot(hid, w3b_ref[...], precision=hp, preferred_element_type=F32) * decay
  row = lax.broadcasted_iota(jnp.int32, kb.shape, 0)
  kb = jnp.where(row == 0, 0.0, kb)
  rs = lax.rsqrt(jnp.sum(kf * kf + kb * kb, axis=0, keepdims=True) + EPS)
  c = kf.shape[1]
  k_ref[:, :c] = kf
  k_ref[:, c:] = kb
  _stage_k_forward(k_ref, mk_ref, u_ref)

  def body(f1, base, k):
    x = jnp.dot(md_ref[f1], _gather_f1(u_ref, base, k), preferred_element_type=F32)
    h_ref[0, f1, pl.ds(0, DFT_F2), :] = (x[:DFT_F2, :c] + x[:DFT_F2, c:]) * rs
    h_ref[0, f1, pl.ds(DFT_F2, DFT_F2), :] = (x[DFT_F2:, :c] - x[DFT_F2:, c:]) * rs

  _grouped_loop(DFT_F1, D_UNROLL, body)


def _hy_filter(zfeat, w1, b1, w2, b2, w3, deltas, mk, md):
  c = HY_CB
  ncb = HY_WIDTH // c
  return pl.pallas_call(
      _hy_filter_kernel,
      grid=(HY_ORDER, ncb),
      in_specs=[_const_spec(zfeat.shape), _const_spec(w1.shape), _const_spec(b1.shape),
                _const_spec(w2.shape), _const_spec(b2.shape),
                pl.BlockSpec((HY_FILT_HIDDEN, c), lambda o, cb: (0, o * 2 * ncb + cb)),
                pl.BlockSpec((HY_FILT_HIDDEN, c), lambda o, cb: (0, o * 2 * ncb + ncb + cb)),
                pl.BlockSpec((1, c), lambda o, cb: (0, cb)),
                _const_spec(mk.shape), _const_spec(md.shape)],
      out_specs=pl.BlockSpec((1, DFT_F1, 2 * DFT_F2, c), lambda o, cb: (o, 0, 0, cb)),
      out_shape=jax.ShapeDtypeStruct((HY_ORDER, DFT_F1, 2 * DFT_F2, HY_WIDTH), F32),
      scratch_shapes=[pltpu.VMEM((SEQ, HY_FILT_HIDDEN), F32),
                      pltpu.VMEM((SEQ, 2 * c), F32),
                      pltpu.VMEM((DFT_GROUPS, UROWS, 2 * c), BF16)],
      compiler_params=pltpu.CompilerParams(
          dimension_semantics=("arbitrary", "arbitrary"), vmem_limit_bytes=VMEM_LIMIT_BYTES),
      name="hy_filter",
  )(zfeat, w1, b1, w2, b2, w3, w3, deltas, mk, md)


def _conv3(p_ref, half, cw_ref, k):
  u = p_ref[pl.ds(half * SEQ, SEQ), :].astype(F32)
  row = lax.broadcasted_iota(jnp.int32, u.shape, 0)
  prev = jnp.where(row == 0, 0.0, pltpu.roll(u, 1, 0))
  nxt = jnp.where(row == SEQ - 1, 0.0, pltpu.roll(u, SEQ - 1, 0))
  w = cw_ref[k]
  return prev * w[0:1] + u * w[1:2] + nxt * w[2:3] + w[3:4]


def _hyena_kernel(pv_ref, pa_ref, pb_ref, cw_ref, skip_ref, h_ref, mk_ref, mki_ref, md_ref,
                  mdi_ref, o_ref, z_ref, y_ref, u_ref, w_ref):
  c = HY_CB
  for half in range(2):
    z_ref[:, half * c:(half + 1) * c] = _conv3(pv_ref, half, cw_ref, 0)
  gates = (pa_ref, pb_ref)
  for order in range(HY_ORDER):
    _stage_k_forward(z_ref, mk_ref, u_ref)

    def spectral(f1, base, k, order=order):
      x = jnp.dot(md_ref[f1], _gather_f1(u_ref, base, k), preferred_element_type=F32)
      xr, xi = x[:DFT_F2], x[DFT_F2:]
      hr = h_ref[order, f1, pl.ds(0, DFT_F2), :]
      hi = h_ref[order, f1, pl.ds(DFT_F2, DFT_F2), :]
      hr = jnp.concatenate([hr, hr], axis=1)
      hi = jnp.concatenate([hi, hi], axis=1)
      y = jnp.concatenate([xr * hr - xi * hi, xr * hi + xi * hr], axis=0).astype(BF16)
      w_ref[_spectrum_slot(base, k)] = y

    _grouped_loop(DFT_F1, D_UNROLL, spectral)

    def inverse_d(f1, base, k):
      w = jnp.dot(mdi_ref[f1], w_ref[_spectrum_slot(base, k)], preferred_element_type=F32)
      _scatter_f1(u_ref, base, k, w)

    _grouped_loop(DFT_F1, D_UNROLL, inverse_d)

    def emit(rows, tile):
      y_ref[rows, :] = tile

    _stage_k_inverse(u_ref, mki_ref, emit)
    for half in range(2):
      lanes = slice(half * c, (half + 1) * c)
      gate = _conv3(gates[order], half, cw_ref, 1 + order)
      z = gate * (y_ref[:, lanes] + skip_ref[order:order + 1, :] * z_ref[:, lanes])
      if order + 1 < HY_ORDER:
        z_ref[:, lanes] = z
      else:
        o_ref[pl.ds(half * SEQ, SEQ), :] = z.astype(BF16)


def _hyena(p, cw, skip, hspec, mk, mki, md, mdi, batch):
  c = HY_CB
  ncb = HY_WIDTH // c
  col = lambda k: pl.BlockSpec((2 * SEQ, c), lambda cb, bp: (bp, k * ncb + cb))
  return pl.pallas_call(
      _hyena_kernel,
      grid=(ncb, batch // 2),
      in_specs=[col(0), col(1), col(2),
                pl.BlockSpec((3, 4, c), lambda cb, bp: (0, 0, cb)),
                pl.BlockSpec((HY_ORDER, c), lambda cb, bp: (0, cb)),
                pl.BlockSpec((HY_ORDER, DFT_F1, 2 * DFT_F2, c), lambda cb, bp: (0, 0, 0, cb),
                             pipeline_mode=pl.Buffered(1)),
                _const_spec(mk.shape), _const_spec(mki.shape), _const_spec(md.shape),
                _const_spec(mdi.shape)],
      out_specs=pl.BlockSpec((2 * SEQ, c), lambda cb, bp: (bp, cb)),
      out_shape=jax.ShapeDtypeStruct((batch * SEQ, HY_WIDTH), BF16),
      scratch_shapes=[pltpu.VMEM((SEQ, 2 * c), F32), pltpu.VMEM((SEQ, 2 * c), F32),
                      pltpu.VMEM((DFT_GROUPS, UROWS, 2 * c), BF16),
                      pltpu.VMEM((DFT_GROUPS, UROWS, 2 * c), BF16)],
      compiler_params=pltpu.CompilerParams(
          dimension_semantics=("arbitrary", "arbitrary"), vmem_limit_bytes=VMEM_LIMIT_BYTES),
      name="hyena",
  )(p, p, p, cw, skip, hspec, mk, mki, md, mdi)


def _t5_bucket(rel):
  half = REL_BUCKETS // 2
  exact = half // 2
  ret = jnp.where(rel > 0, half, 0)
  n = jnp.abs(rel)
  nf = jnp.maximum(n, 1).astype(F32)
  large = exact + (jnp.log(nf / exact) / math.log(REL_MAX_DISTANCE / exact) * (half - exact)).astype(jnp.int32)
  large = jnp.minimum(large, half - 1)
  return ret + jnp.where(n < exact, n, large)


def _bucket_maps():
  qi = jnp.arange(ATT_BQ)[:, None]
  ki = jnp.arange(ATT_BK)[None, :]
  rel = ki - N_SIDE - qi
  maps = [jnp.where(jnp.abs(rel) <= N_SIDE, _t5_bucket(rel * d), -1) for d in DILATIONS]
  return jnp.stack(maps).astype(jnp.int32)


def _dil_attn_kernel(tab_ref, *refs):
  qkv_refs = refs[:9]
  bkt_ref, o_ref = refs[9:11]
  ks_ref, vs_ref, acc_ref, m_ref, l_ref, bias_ref = refs[11:]
  pair = pl.program_id(1)
  lane_lo = lax.broadcasted_iota(jnp.int32, (1, LANES), 1) < HEAD_DIM
  key_idx = lax.broadcasted_iota(jnp.int32, (1, ATT_BK), 1)

  @pl.when((pl.program_id(0) == 0) & (pair == 0))
  def _():
    for g in range(N_GROUPS):
      bkt = bkt_ref[g]
      outside = jnp.where(bkt < 0, NEG, 0.0)
      for c in range(HEADS_PER_GROUP):
        b = outside
        for t in range(REL_BUCKETS):
          b = b + jnp.where(bkt == t, tab_ref[t, g * HEADS_PER_GROUP + c], 0.0)
        bias_ref[g * HEADS_PER_GROUP + c] = b

  group_order = sorted(range(N_GROUPS), key=lambda g: -DILATIONS[g])
  for g in group_order:
    d = DILATIONS[g]
    q_ref, k_ref, v_ref = qkv_refs[3 * g:3 * g + 3]
    m_sub = SEQ // d
    span = m_sub + 2 * N_SIDE
    blocks = m_sub // ATT_BQ

    def residue_rows(ref, r, start, size, d=d):
      if d == 1:
        return ref[pl.ds(start, size), :]
      return ref[0, r, pl.ds(start, size), :]

    zeros = jnp.zeros((N_SIDE, LANES), BF16)
    for r in range(d):
      for src, dst in ((k_ref, ks_ref), (v_ref, vs_ref)):
        dst[pl.ds(r * span, N_SIDE), :] = zeros
        dst[pl.ds(r * span + N_SIDE, m_sub), :] = residue_rows(src, r, 0, m_sub)
        dst[pl.ds(r * span + N_SIDE + m_sub, N_SIDE), :] = zeros

    def block(t, carry, d=d, m_sub=m_sub, g=g, span=span, blocks=blocks,
              residue_rows=residue_rows, q_ref=q_ref):
      r = t // blocks
      i = t % blocks
      q = residue_rows(q_ref, r, pl.multiple_of(i * ATT_BQ, ATT_BQ), ATT_BQ)
      key_rows = pl.ds(pl.multiple_of(r * span + i * ATT_BQ, ATT_BQ), ATT_BK)
      k = ks_ref[key_rows, :]
      v = vs_ref[key_rows, :]
      start = r + d * ATT_BQ * i
      rows_q = pl.ds(start, ATT_BQ) if d == 1 else pl.ds(start, ATT_BQ, stride=d)
      kpos = ATT_BQ * i - N_SIDE + key_idx
      kvalid = (kpos >= 0) & (kpos < m_sub)
      zero = jnp.zeros_like(q)
      outs, maxs, sums = [], [], []
      for h in range(2):
        qh = jnp.where(lane_lo if h == 0 else jnp.logical_not(lane_lo), q, zero)
        s = lax.dot_general(qh, k, (((1,), (1,)), ((), ())), preferred_element_type=F32)
        s = jnp.where(kvalid, s + bias_ref[g * HEADS_PER_GROUP + 2 * pair + h], NEG)
        mx = jnp.max(s, axis=-1, keepdims=True)
        p = jnp.exp(s - mx)
        sums.append(jnp.sum(p, axis=-1, keepdims=True))
        maxs.append(mx)
        outs.append(jnp.dot(p.astype(BF16), v, preferred_element_type=F32))
      o = jnp.where(lane_lo, outs[0], outs[1])
      mx = jnp.where(lane_lo, maxs[0], maxs[1])
      sm = jnp.where(lane_lo, sums[0], sums[1])
      if g == group_order[0]:
        acc_ref[rows_q, :] = o
        m_ref[rows_q, :] = mx
        l_ref[rows_q, :] = sm
      else:
        m_old = m_ref[rows_q, :]
        m_new = jnp.maximum(m_old, mx)
        a_old = jnp.exp(m_old - m_new)
        a_new = jnp.exp(mx - m_new)
        acc_ref[rows_q, :] = acc_ref[rows_q, :] * a_old + o * a_new
        l_ref[rows_q, :] = l_ref[rows_q, :] * a_old + sm * a_new
        m_ref[rows_q, :] = m_new
      return carry

    lax.fori_loop(0, SEQ // ATT_BQ, block, 0, unroll=ATT_UNROLL)

  o_ref[...] = (acc_ref[...] / l_ref[...]).astype(BF16)


def _dil_attn(groups, rel_bias, bkt, batch):
  per_group = ATT_OUT // LANES

  def col(g, kind):
    d = DILATIONS[g]
    if d == 1:
      return pl.BlockSpec((SEQ, LANES), lambda b, pr, tab: (b, kind * per_group + pr))
    return pl.BlockSpec((1, d, SEQ // d, LANES),
                        lambda b, pr, tab: (b, 0, 0, kind * per_group + pr))

  in_specs = [col(g, kind) for g in range(N_GROUPS) for kind in range(3)]
  in_specs += [pl.BlockSpec(bkt.shape, lambda b, pr, tab: (0, 0, 0))]
  key_rows = SEQ + 2 * N_SIDE * max(DILATIONS)
  grid_spec = pltpu.PrefetchScalarGridSpec(
      num_scalar_prefetch=1,
      grid=(batch, per_group),
      in_specs=in_specs,
      out_specs=pl.BlockSpec((SEQ, LANES), lambda b, pr, tab: (b, pr)),
      scratch_shapes=[pltpu.VMEM((key_rows, LANES), BF16), pltpu.VMEM((key_rows, LANES), BF16),
                      pltpu.VMEM((SEQ, LANES), F32), pltpu.VMEM((SEQ, LANES), F32),
                      pltpu.VMEM((SEQ, LANES), F32),
                      pltpu.VMEM((N_GROUPS * HEADS_PER_GROUP, ATT_BQ, ATT_BK), F32)])
  operands = [groups[g] for g in range(N_GROUPS) for _ in range(3)]
  return pl.pallas_call(
      _dil_attn_kernel,
      grid_spec=grid_spec,
      out_shape=jax.ShapeDtypeStruct((batch * SEQ, ATT_OUT), BF16),
      compiler_params=pltpu.CompilerParams(
          dimension_semantics=("arbitrary", "arbitrary"), vmem_limit_bytes=VMEM_LIMIT_BYTES),
      name="dil_attn",
  )(rel_bias, *operands, bkt)


def _filter_features():
  t = jnp.linspace(0.0, 1.0, SEQ, dtype=F32)[:, None]
  w = (2.0 * math.pi / SEQ) * jnp.arange(SEQ, dtype=F32)[:, None]
  f = jnp.linspace(1e-4, HY_POS_BANDS - 1, HY_POS_BANDS, dtype=F32)[None]
  z = jnp.concatenate([t, jnp.cos(f * w), -jnp.sin(f * w)], axis=-1)
  return jnp.pad(z, ((0, 0), (0, LANES - HY_EMB)))


def _decay_rates():
  max_decay = math.log(HY_DECAY_TARGET) / HY_FAST_DECAY_PCT
  min_decay = math.log(HY_DECAY_TARGET) / HY_SLOW_DECAY_PCT
  return jnp.abs(jnp.linspace(min_decay, max_decay, HY_WIDTH, dtype=F32))[None]


def kernel(x, rel_bias, ffn1_norm, ffn1_w_gate, ffn1_w_up, ffn1_w_down, mix_norm, w_in, w_gate, b_gate, hy_conv_w, hy_conv_b, hy_filt_w1, hy_filt_b1, hy_filt_w2, hy_filt_b2, hy_filt_w3, hy_skip, q_norm, k_norm, w_hy_proj, w_at_proj, w_out, ffn2_norm, ffn2_w_gate, ffn2_w_up, ffn2_w_down):
  batch, seq, d_model = x.shape
  assert (seq, d_model) == (SEQ, D_MODEL) and batch % 2 == 0
  depth = ffn1_norm.shape[0]
  mk, mki, md, mdi = _dft_constants()
  zfeat = _filter_features()
  deltas = _decay_rates()
  bkt = _bucket_maps()
  bf = lambda w: w.astype(BF16)
  row = lambda v: v.reshape(1, -1)

  xt = x.reshape(batch * seq, d_model)
  for l in range(depth):
    xt, p, *groups = _ffn_proj(
        xt, row(ffn1_norm[l]), bf(ffn1_w_gate[l]), bf(ffn1_w_up[l]), bf(ffn1_w_down[l]),
        row(mix_norm[l]), bf(w_in[l]), row(jnp.tile(q_norm[l], 2)),
        row(jnp.tile(k_norm[l], 2)), batch)
    w1 = jnp.pad(hy_filt_w1[l], ((0, LANES - HY_EMB), (0, 0)))
    hspec = _hy_filter(zfeat, w1, row(hy_filt_b1[l]), hy_filt_w2[l], row(hy_filt_b2[l]),
                       hy_filt_w3[l], deltas, mk, md)
    cw = jnp.concatenate([hy_conv_w[l], hy_conv_b[l][None]], axis=0)
    cw = cw.reshape(4, 3, HY_WIDTH).transpose(1, 0, 2)
    yhy = _hyena(p, cw, hy_skip[l], hspec, mk, mki, md, mdi, batch)
    yat = _dil_attn(groups, rel_bias, bkt, batch)
    xt = _merge_ffn(xt, yhy, yat, row(mix_norm[l]), bf(w_gate[l]), row(b_gate[l]),
                    bf(w_hy_proj[l]), bf(w_at_proj[l]), bf(w_out[l]), row(ffn2_norm[l]),
                    bf(ffn2_w_gate[l]), bf(ffn2_w_up[l]), bf(ffn2_w_down[l]))
  return xt.reshape(batch, seq, d_model)
```

```python
import functools
import math

import jax
import jax.numpy as jnp
import numpy as np
from jax import lax
from jax.experimental import pallas as pl
from jax.experimental.pallas import tpu as pltpu

F32 = jnp.float32
BF16 = jnp.bfloat16

D_MODEL = 1024
SEQ = 4096
HEAD_DIM = 64
HY_WIDTH = 512
HY_ORDER = 2
HY_FILT_HIDDEN = 64
HY_POS_BANDS = 16
HY_EMB = 1 + 2 * HY_POS_BANDS
HY_FAST_DECAY_PCT = 0.3
HY_SLOW_DECAY_PCT = 1.5
HY_DECAY_TARGET = 1e-2
WINDOWS = (128, 512, 2048)
DILATIONS = (1, 4, 16)
N_GROUPS = 3
HEADS_PER_GROUP = 4
ATT_WIDTH = N_GROUPS * HEADS_PER_GROUP * HEAD_DIM
ATT_OUT = HEADS_PER_GROUP * HEAD_DIM
REL_BUCKETS = 32
REL_MAX_DISTANCE = 1024
D_FF = (8 * D_MODEL // 3) // 128 * 128
IN_WIDTH = 3 * HY_WIDTH + 3 * ATT_WIDTH
EPS = 1e-6
NEG = -1e30

SUBLANES = 8
LANES = 128
BF16_ROWS = 16
VMEM_LIMIT_BYTES = 60 * 1024 * 1024

DFT_N = 2 * SEQ
DFT_N2 = 128
DFT_N1 = SEQ // DFT_N2
DFT_F1 = 64
DFT_F2 = SEQ // DFT_F1
DFT_GROUPS = DFT_N2 // SUBLANES
KROWS = DFT_N1 * SUBLANES
UROWS = 2 * DFT_F1 * SUBLANES

N_SIDE = 64
ATT_BQ = 128
ATT_BK = ATT_BQ + 2 * N_SIDE

FFN_MAIN = D_FF // 256 * 256
FFN_CHUNKS = ((0, FFN_MAIN // 2), (FFN_MAIN // 2, FFN_MAIN))
TOKEN_TILE = 512
TOKEN_PARTS = 2
MERGE_TILE = 1024
HY_CB = 128
K_UNROLL = 8
D_UNROLL = 16
ATT_UNROLL = 16


def _const_spec(shape):
  nd = len(shape)
  return pl.BlockSpec(shape, lambda *_: (0,) * nd, pipeline_mode=pl.Buffered(1))


def _row_parts(rows):
  part = TOKEN_TILE // TOKEN_PARTS
  return [pl.ds(i * part, part) for i in range(rows // part)]


def _rms(x, g):
  return x * lax.rsqrt(jnp.mean(x * x, axis=-1, keepdims=True) + EPS) * g


def _pair_tail_columns(wg_ref, wu_ref, tail_ref):
  @pl.when(pl.program_id(0) == 0)
  def _():
    tail_ref[:, :D_FF - FFN_MAIN] = wg_ref[:, FFN_MAIN:]
    tail_ref[:, D_FF - FFN_MAIN:] = wu_ref[:, FFN_MAIN:]


def _swiglu_residual(x, g_ref, wg_ref, wu_ref, wd_ref, tail_ref):
  h = _rms(x, g_ref[...]).astype(BF16)
  acc = jnp.zeros_like(x)
  for c0, c1 in FFN_CHUNKS:
    a = jnp.dot(h, wg_ref[:, c0:c1], preferred_element_type=F32)
    u = jnp.dot(h, wu_ref[:, c0:c1], preferred_element_type=F32)
    s = (a * jax.nn.sigmoid(a) * u).astype(BF16)
    if c1 == FFN_MAIN:
      t = jnp.dot(h, tail_ref[...], preferred_element_type=F32)
      a, u = t[:, :D_FF - FFN_MAIN], t[:, D_FF - FFN_MAIN:]
      s = jnp.concatenate([s, (a * jax.nn.sigmoid(a) * u).astype(BF16)], axis=1)
      c1 = D_FF
    acc = acc + jnp.dot(s, wd_ref[c0:c1, :], preferred_element_type=F32)
  return x + 0.5 * acc


def _head_norm(x, gain, lane_lo):
  sq = x * x
  s_lo = jnp.sum(jnp.where(lane_lo, sq, 0.0), axis=-1, keepdims=True)
  s_all = jnp.sum(sq, axis=-1, keepdims=True)
  ms = jnp.where(lane_lo, s_lo, s_all - s_lo) * (1.0 / HEAD_DIM)
  return x * lax.rsqrt(ms + EPS) * gain


def _ffn_proj_kernel(x_ref, g1_ref, wg_ref, wu_ref, wd_ref, gm_ref, win_ref, qg_ref, kg_ref,
                     x1_ref, hy_ref, a0_ref, a1_ref, a2_ref, perm_ref, tail_ref):
  _pair_tail_columns(wg_ref, wu_ref, tail_ref)
  lane_lo = lax.broadcasted_iota(jnp.int32, (1, LANES), 1) < HEAD_DIM
  kind_gain = (qg_ref[...] * (HEAD_DIM ** -0.5), kg_ref[...], None)
  group_out = (a0_ref, a1_ref, a2_ref)
  for part, rows in enumerate(_row_parts(x_ref.shape[0])):
    x1 = _swiglu_residual(x_ref[rows, :], g1_ref, wg_ref, wu_ref, wd_ref, tail_ref)
    x1_ref[rows, :] = x1
    h = _rms(x1, gm_ref[...]).astype(BF16)
    n_rows = x1.shape[0]
    for c0 in range(0, 3 * HY_WIDTH, ATT_WIDTH):
      hy_ref[rows, c0:c0 + ATT_WIDTH] = jnp.dot(
          h, win_ref[:, c0:c0 + ATT_WIDTH], preferred_element_type=F32).astype(BF16)
    n_slabs = ATT_WIDTH // LANES
    for g, d in enumerate(DILATIONS):
      buf = 2 * (g - 1) + part
      for kind, gain in enumerate(kind_gain):
        c0 = 3 * HY_WIDTH + kind * ATT_WIDTH + g * ATT_OUT
        y = jnp.dot(h, win_ref[:, c0:c0 + ATT_OUT], preferred_element_type=F32)
        for s0 in range(0, ATT_OUT, LANES):
          ys = y[:, s0:s0 + LANES]
          ys = ys if gain is None else _head_norm(ys, gain, lane_lo)
          if d == 1:
            a0_ref[rows, kind * ATT_OUT + s0:kind * ATT_OUT + s0 + LANES] = ys.astype(BF16)
          else:
            perm_ref[buf, (kind * ATT_OUT + s0) // LANES] = ys
      if d == 1:
        continue
      per_res = n_rows // d
      for r in range(d):
        piece = jnp.concatenate(
            [perm_ref[buf, sl, pl.ds(r, per_res, stride=d), :] for sl in range(n_slabs)], axis=1)
        group_out[g][0, r, pl.ds(part * per_res, per_res), :] = piece.astype(BF16)


def _ffn_proj(x, g1, wg, wu, wd, gm, win, qg, kg, batch):
  t = x.shape[0]
  tm = TOKEN_TILE
  tiles = SEQ // tm
  tok = lambda w: pl.BlockSpec((tm, w), lambda i: (i, 0))
  res = lambda d: pl.BlockSpec((1, d, tm // d, ATT_WIDTH), lambda i: (i // tiles, 0, i % tiles, 0))
  res_shape = lambda d: jax.ShapeDtypeStruct((batch, d, SEQ // d, ATT_WIDTH), BF16)
  d1, d2 = DILATIONS[1], DILATIONS[2]
  return pl.pallas_call(
      _ffn_proj_kernel,
      grid=(t // tm,),
      in_specs=[tok(D_MODEL), _const_spec((1, D_MODEL)), _const_spec((D_MODEL, D_FF)),
                _const_spec((D_MODEL, D_FF)), _const_spec((D_FF, D_MODEL)),
                _const_spec((1, D_MODEL)), _const_spec((D_MODEL, IN_WIDTH)),
                _const_spec((1, LANES)), _const_spec((1, LANES))],
      out_specs=[tok(D_MODEL), tok(3 * HY_WIDTH), tok(ATT_WIDTH), res(d1), res(d2)],
      out_shape=[jax.ShapeDtypeStruct((t, D_MODEL), F32),
                 jax.ShapeDtypeStruct((t, 3 * HY_WIDTH), BF16),
                 jax.ShapeDtypeStruct((t, ATT_WIDTH), BF16), res_shape(d1), res_shape(d2)],
      scratch_shapes=[pltpu.VMEM((2 * TOKEN_PARTS, ATT_WIDTH // LANES, tm // TOKEN_PARTS, LANES),
                                 F32),
                      pltpu.VMEM((D_MODEL, 2 * (D_FF - FFN_MAIN)), BF16)],
      compiler_params=pltpu.CompilerParams(
          dimension_semantics=("arbitrary",), vmem_limit_bytes=VMEM_LIMIT_BYTES),
      name="ffn_proj",
  )(x, g1, wg, wu, wd, gm, win, qg, kg)


def _merge_ffn_kernel(x_ref, yhy_ref, yat_ref, gm_ref, wgate_ref, bgate_ref, why_ref, wat_ref,
                      wout_ref, g2_ref, wg_ref, wu_ref, wd_ref, o_ref, tail_ref):
  _pair_tail_columns(wg_ref, wu_ref, tail_ref)
  for rows in _row_parts(x_ref.shape[0]):
    x = x_ref[rows, :]
    h = _rms(x, gm_ref[...]).astype(BF16)
    gates = jax.nn.sigmoid(
        jnp.dot(h, wgate_ref[...], preferred_element_type=F32) + bgate_ref[...])
    a = jnp.dot(yhy_ref[rows, :], why_ref[...], preferred_element_type=F32)
    b = jnp.dot(yat_ref[rows, :], wat_ref[...], preferred_element_type=F32)
    y = gates[:, :D_MODEL] * a + gates[:, D_MODEL:] * b
    x2 = x + jnp.dot(y.astype(BF16), wout_ref[...], preferred_element_type=F32)
    o_ref[rows, :] = _swiglu_residual(x2, g2_ref, wg_ref, wu_ref, wd_ref, tail_ref)


def _merge_ffn(x, yhy, yat, gm, wgate, bgate, why, wat, wout, g2, wg, wu, wd):
  t = x.shape[0]
  tm = MERGE_TILE
  tok = lambda w: pl.BlockSpec((tm, w), lambda i: (i, 0))
  return pl.pallas_call(
      _merge_ffn_kernel,
      grid=(t // tm,),
      in_specs=[tok(D_MODEL), tok(HY_WIDTH), tok(ATT_OUT), _const_spec((1, D_MODEL)),
                _const_spec((D_MODEL, 2 * D_MODEL)), _const_spec((1, 2 * D_MODEL)),
                _const_spec((HY_WIDTH, D_MODEL)), _const_spec((ATT_OUT, D_MODEL)),
                _const_spec((D_MODEL, D_MODEL)), _const_spec((1, D_MODEL)),
                _const_spec((D_MODEL, D_FF)), _const_spec((D_MODEL, D_FF)),
                _const_spec((D_FF, D_MODEL))],
      out_specs=tok(D_MODEL),
      out_shape=jax.ShapeDtypeStruct((t, D_MODEL), F32),
      scratch_shapes=[pltpu.VMEM((D_MODEL, 2 * (D_FF - FFN_MAIN)), BF16)],
      compiler_params=pltpu.CompilerParams(
          dimension_semantics=("arbitrary",), vmem_limit_bytes=VMEM_LIMIT_BYTES),
      name="merge_ffn",
  )(x, yhy, yat, gm, wgate, bgate, why, wat, wout, g2, wg, wu, wd)


@functools.lru_cache(maxsize=None)
def _dft_constants_host():
  f1 = np.arange(DFT_F1)
  n1 = np.arange(DFT_N1)
  f2 = np.arange(DFT_F2)
  n2 = np.arange(DFT_N2)
  eye = np.eye(SUBLANES)
  a_ang = 2.0 * np.pi * (((2 * f1 + 1)[:, None] * n1[None, :]) % DFT_N2) / DFT_N2
  ar, ai = np.cos(a_ang), -np.sin(a_ang)
  mk = np.stack([np.einsum("fn,st->fsnt", ar, eye), np.einsum("fn,st->fsnt", ai, eye)], axis=1)
  mk = mk.reshape(UROWS, KROWS)
  scale = 2.0 / DFT_N
  mki = np.stack([np.einsum("fn,st->ntfs", ar * scale, eye),
                  np.einsum("fn,st->ntfs", ai * scale, eye)], axis=3)
  mki = mki.reshape(KROWS, UROWS)
  ph = (n2[None, None, :] * (DFT_N2 * f2[None, :, None] + 2 * f1[:, None, None] + 1)) % (2 * DFT_N)
  d_ang = 2.0 * np.pi * ph / (2 * DFT_N)
  dr = np.cos(d_ang).reshape(DFT_F1, DFT_F2, DFT_GROUPS, SUBLANES)
  di = -np.sin(d_ang).reshape(DFT_F1, DFT_F2, DFT_GROUPS, SUBLANES)
  top = np.stack([dr, -di], axis=3)
  bot = np.stack([di, dr], axis=3)
  md = np.stack([top, bot], axis=1).reshape(DFT_F1, 2 * DFT_F2, 2 * DFT_N2)
  drt = dr.transpose(0, 2, 3, 1)
  dit = di.transpose(0, 2, 3, 1)
  re_rows = np.stack([drt, dit], axis=3)
  im_rows = np.stack([-dit, drt], axis=3)
  mdi = np.stack([re_rows, im_rows], axis=2).reshape(DFT_F1, 2 * DFT_N2, 2 * DFT_F2)
  return tuple(np.asarray(m, np.float32) for m in (mk, mki, md, mdi))


def _dft_constants():
  return tuple(jnp.asarray(m, dtype=BF16) for m in _dft_constants_host())


def _grouped_loop(count, group, body):
  def outer(i, carry):
    base = i * group
    for k in range(group):
      body(base + k, base, k)
    return carry

  lax.fori_loop(0, count // group, outer, 0)


def _tile_rows(base, scale, offset, size):
  return pl.ds(pl.multiple_of(base * scale, size) + offset, size)


def _stage_k_forward(z_ref, mk_ref, u_ref):
  def body(j, base, k):
    tiles = [z_ref[_tile_rows(base, SUBLANES, DFT_N2 * n1 + SUBLANES * k, SUBLANES), :]
             for n1 in range(DFT_N1)]
    rhs = jnp.concatenate(tiles, axis=0).astype(BF16)
    u_ref[j] = jnp.dot(mk_ref[...], rhs, preferred_element_type=F32).astype(BF16)

  _grouped_loop(DFT_GROUPS, K_UNROLL, body)


def _gather_f1(u_ref, base, k):
  rows = _tile_rows(base, BF16_ROWS, BF16_ROWS * k, BF16_ROWS)
  return jnp.concatenate([u_ref[j, rows, :] for j in range(DFT_GROUPS)], axis=0)


def _scatter_f1(u_ref, base, k, w):
  rows = _tile_rows(base, BF16_ROWS, BF16_ROWS * k, BF16_ROWS)
  wb = w.astype(BF16)
  for j in range(DFT_GROUPS):
    u_ref[j, rows, :] = wb[BF16_ROWS * j:BF16_ROWS * (j + 1), :]


def _spectrum_slot(base, k):
  per_group = UROWS // (2 * DFT_F2)
  return (base // per_group + k // per_group, pl.ds((k % per_group) * 2 * DFT_F2, 2 * DFT_F2),
          slice(None))


def _stage_k_inverse(u_ref, mki_ref, emit):
  def body(j, base, k):
    y = jnp.dot(mki_ref[...], u_ref[j], preferred_element_type=F32)
    for n1 in range(DFT_N1):
      rows = _tile_rows(base, SUBLANES, DFT_N2 * n1 + SUBLANES * k, SUBLANES)
      emit(rows, y[SUBLANES * n1:SUBLANES * (n1 + 1), :])

  _grouped_loop(DFT_GROUPS, K_UNROLL, body)


def _hy_filter_kernel(z_ref, w1_ref, b1_ref, w2_ref, b2_ref, w3f_ref, w3b_ref, dl_ref,
                      mk_ref, md_ref, h_ref, hid_ref, k_ref, u_ref):
  hp = lax.Precision.HIGHEST

  @pl.when((pl.program_id(0) == 0) & (pl.program_id(1) == 0))
  def _():
    hid = jnp.sin(
        jnp.dot(z_ref[...], w1_ref[...], precision=hp, preferred_element_type=F32) + b1_ref[...])
    hid_ref[...] = jnp.sin(
        jnp.dot(hid, w2_ref[...], precision=hp, preferred_element_type=F32) + b2_ref[...])

  hid = hid_ref[...]
  decay = jnp.exp(-z_ref[:, 0:1] * dl_ref[...])
  kf = jnp.dot(hid, w3f_ref[...], precision=hp, preferred_element_type=F32) * decay
  kb = jnp.dot(hid, w3b_ref[...], precision=hp, preferred_element_type=F32) * decay
  row = lax.broadcasted_iota(jnp.int32, kb.shape, 0)
  kb = jnp.where(row == 0, 0.0, kb)
  rs = lax.rsqrt(jnp.sum(kf * kf + kb * kb, axis=0, keepdims=True) + EPS)
  c = kf.shape[1]
  k_ref[:, :c] = kf
  k_ref[:, c:] = kb
  _stage_k_forward(k_ref, mk_ref, u_ref)

  def body(f1, base, k):
    x = jnp.dot(md_ref[f1], _gather_f1(u_ref, base, k), preferred_element_type=F32)
    h_ref[0, f1, pl.ds(0, DFT_F2), :] = (x[:DFT_F2, :c] + x[:DFT_F2, c:]) * rs
    h_ref[0, f1, pl.ds(DFT_F2, DFT_F2), :] = (x[DFT_F2:, :c] - x[DFT_F2:, c:]) * rs

  _grouped_loop(DFT_F1, D_UNROLL, body)


def _hy_filter(zfeat, w1, b1, w2, b2, w3, deltas, mk, md):
  c = HY_CB
  ncb = HY_WIDTH // c
  return pl.pallas_call(
      _hy_filter_kernel,
      grid=(HY_ORDER, ncb),
      in_specs=[_const_spec(zfeat.shape), _const_spec(w1.shape), _const_spec(b1.shape),
                _const_spec(w2.shape), _const_spec(b2.shape),
                pl.BlockSpec((HY_FILT_HIDDEN, c), lambda o, cb: (0, o * 2 * ncb + cb)),
                pl.BlockSpec((HY_FILT_HIDDEN, c), lambda o, cb: (0, o * 2 * ncb + ncb + cb)),
                pl.BlockSpec((1, c), lambda o, cb: (0, cb)),
                _const_spec(mk.shape), _const_spec(md.shape)],
      out_specs=pl.BlockSpec((1, DFT_F1, 2 * DFT_F2, c), lambda o, cb: (o, 0, 0, cb)),
      out_shape=jax.ShapeDtypeStruct((HY_ORDER, DFT_F1, 2 * DFT_F2, HY_WIDTH), F32),
      scratch_shapes=[pltpu.VMEM((SEQ, HY_FILT_HIDDEN), F32),
                      pltpu.VMEM((SEQ, 2 * c), F32),
                      pltpu.VMEM((DFT_GROUPS, UROWS, 2 * c), BF16)],
      compiler_params=pltpu.CompilerParams(
          dimension_semantics=("arbitrary", "arbitrary"), vmem_limit_bytes=VMEM_LIMIT_BYTES),
      name="hy_filter",
  )(zfeat, w1, b1, w2, b2, w3, w3, deltas, mk, md)


def _conv3(p_ref, half, cw_ref, k):
  u = p_ref[pl.ds(half * SEQ, SEQ), :].astype(F32)
  row = lax.broadcasted_iota(jnp.int32, u.shape, 0)
  prev = jnp.where(row == 0, 0.0, pltpu.roll(u, 1, 0))
  nxt = jnp.where(row == SEQ - 1, 0.0, pltpu.roll(u, SEQ - 1, 0))
  w = cw_ref[k]
  return prev * w[0:1] + u * w[1:2] + nxt * w[2:3] + w[3:4]


def _hyena_kernel(pv_ref, pa_ref, pb_ref, cw_ref, skip_ref, h_ref, mk_ref, mki_ref, md_ref,
                  mdi_ref, o_ref, z_ref, y_ref, u_ref, w_ref):
  c = HY_CB
  for half in range(2):
    z_ref[:, half * c:(half + 1) * c] = _conv3(pv_ref, half, cw_ref, 0)
  gates = (pa_ref, pb_ref)
  for order in range(HY_ORDER):
    _stage_k_forward(z_ref, mk_ref, u_ref)

    def spectral(f1, base, k, order=order):
      x = jnp.dot(md_ref[f1], _gather_f1(u_ref, base, k), preferred_element_type=F32)
      xr, xi = x[:DFT_F2], x[DFT_F2:]
      hr = h_ref[order, f1, pl.ds(0, DFT_F2), :]
      hi = h_ref[order, f1, pl.ds(DFT_F2, DFT_F2), :]
      hr = jnp.concatenate([hr, hr], axis=1)
      hi = jnp.concatenate([hi, hi], axis=1)
      y = jnp.concatenate([xr * hr - xi * hi, xr * hi + xi * hr], axis=0).astype(BF16)
      w_ref[_spectrum_slot(base, k)] = y

    _grouped_loop(DFT_F1, D_UNROLL, spectral)

    def inverse_d(f1, base, k):
      w = jnp.dot(mdi_ref[f1], w_ref[_spectrum_slot(base, k)], preferred_element_type=F32)
      _scatter_f1(u_ref, base, k, w)

    _grouped_loop(DFT_F1, D_UNROLL, inverse_d)

    def emit(rows, tile):
      y_ref[rows, :] = tile

    _stage_k_inverse(u_ref, mki_ref, emit)
    for half in range(2):
      lanes = slice(half * c, (half + 1) * c)
      gate = _conv3(gates[order], half, cw_ref, 1 + order)
      z = gate * (y_ref[:, lanes] + skip_ref[order:order + 1, :] * z_ref[:, lanes])
      if order + 1 < HY_ORDER:
        z_ref[:, lanes] = z
      else:
        o_ref[pl.ds(half * SEQ, SEQ), :] = z.astype(BF16)


def _hyena(p, cw, skip, hspec, mk, mki, md, mdi, batch):
  c = HY_CB
  ncb = HY_WIDTH // c
  col = lambda k: pl.BlockSpec((2 * SEQ, c), lambda cb, bp: (bp, k * ncb + cb))
  return pl.pallas_call(
      _hyena_kernel,
      grid=(ncb, batch // 2),
      in_specs=[col(0), col(1), col(2),
                pl.BlockSpec((3, 4, c), lambda cb, bp: (0, 0, cb)),
                pl.BlockSpec((HY_ORDER, c), lambda cb, bp: (0, cb)),
                pl.BlockSpec((HY_ORDER, DFT_F1, 2 * DFT_F2, c), lambda cb, bp: (0, 0, 0, cb),
                             pipeline_mode=pl.Buffered(1)),
                _const_spec(mk.shape), _const_spec(mki.shape), _const_spec(md.shape),
                _const_spec(mdi.shape)],
      out_specs=pl.BlockSpec((2 * SEQ, c), lambda cb, bp: (bp, cb)),
      out_shape=jax.ShapeDtypeStruct((batch * SEQ, HY_WIDTH), BF16),
      scratch_shapes=[pltpu.VMEM((SEQ, 2 * c), F32), pltpu.VMEM((SEQ, 2 * c), F32),
                      pltpu.VMEM((DFT_GROUPS, UROWS, 2 * c), BF16),
                      pltpu.VMEM((DFT_GROUPS, UROWS, 2 * c), BF16)],
      compiler_params=pltpu.CompilerParams(
          dimension_semantics=("arbitrary", "arbitrary"), vmem_limit_bytes=VMEM_LIMIT_BYTES),
      name="hyena",
  )(p, p, p, cw, skip, hspec, mk, mki, md, mdi)


def _t5_bucket(rel):
  half = REL_BUCKETS // 2
  exact = half // 2
  ret = jnp.where(rel > 0, half, 0)
  n = jnp.abs(rel)
  nf = jnp.maximum(n, 1).astype(F32)
  large = exact + (jnp.log(nf / exact) / math.log(REL_MAX_DISTANCE / exact) * (half - exact)).astype(jnp.int32)
  large = jnp.minimum(large, half - 1)
  return ret + jnp.where(n < exact, n, large)


def _bucket_maps():
  qi = jnp.arange(ATT_BQ)[:, None]
  ki = jnp.arange(ATT_BK)[None, :]
  rel = ki - N_SIDE - qi
  maps = [jnp.where(jnp.abs(rel) <= N_SIDE, _t5_bucket(rel * d), -1) for d in DILATIONS]
  return jnp.stack(maps).astype(jnp.int32)


def _dil_attn_kernel(tab_ref, *refs):
  qkv_refs = refs[:9]
  bkt_ref, o_ref = refs[9:11]
  ks_ref, vs_ref, acc_ref, m_ref, l_ref, bias_ref = refs[11:]
  pair = pl.program_id(1)
  lane_lo = lax.broadcasted_iota(jnp.int32, (1, LANES), 1) < HEAD_DIM
  key_idx = lax.broadcasted_iota(jnp.int32, (1, ATT_BK), 1)

  @pl.when((pl.program_id(0) == 0) & (pair == 0))
  def _():
    for g in range(N_GROUPS):
      bkt = bkt_ref[g]
      outside = jnp.where(bkt < 0, NEG, 0.0)
      for c in range(HEADS_PER_GROUP):
        b = outside
        for t in range(REL_BUCKETS):
          b = b + jnp.where(bkt == t, tab_ref[t, g * HEADS_PER_GROUP + c], 0.0)
        bias_ref[g * HEADS_PER_GROUP + c] = b

  group_order = sorted(range(N_GROUPS), key=lambda g: -DILATIONS[g])
  for g in group_order:
    d = DILATIONS[g]
    q_ref, k_ref, v_ref = qkv_refs[3 * g:3 * g + 3]
    m_sub = SEQ // d
    span = m_sub + 2 * N_SIDE
    blocks = m_sub // ATT_BQ

    def residue_rows(ref, r, start, size, d=d):
      if d == 1:
        return ref[pl.ds(start, size), :]
      return ref[0, r, pl.ds(start, size), :]

    zeros = jnp.zeros((N_SIDE, LANES), BF16)
    for r in range(d):
      for src, dst in ((k_ref, ks_ref), (v_ref, vs_ref)):
        dst[pl.ds(r * span, N_SIDE), :] = zeros
        dst[pl.ds(r * span + N_SIDE, m_sub), :] = residue_rows(src, r, 0, m_sub)
        dst[pl.ds(r * span + N_SIDE + m_sub, N_SIDE), :] = zeros

    def block(t, carry, d=d, m_sub=m_sub, g=g, span=span, blocks=blocks,
              residue_rows=residue_rows, q_ref=q_ref):
      r = t // blocks
      i = t % blocks
      q = residue_rows(q_ref, r, pl.multiple_of(i * ATT_BQ, ATT_BQ), ATT_BQ)
      key_rows = pl.ds(pl.multiple_of(r * span + i * ATT_BQ, ATT_BQ), ATT_BK)
      k = ks_ref[key_rows, :]
      v = vs_ref[key_rows, :]
      start = r + d * ATT_BQ * i
      rows_q = pl.ds(start, ATT_BQ) if d == 1 else pl.ds(start, ATT_BQ, stride=d)
      kpos = ATT_BQ * i - N_SIDE + key_idx
      kvalid = (kpos >= 0) & (kpos < m_sub)
      zero = jnp.zeros_like(q)
      outs, maxs, sums = [], [], []
      for h in range(2):
        qh = jnp.where(lane_lo if h == 0 else jnp.logical_not(lane_lo), q, zero)
        s = lax.dot_general(qh, k, (((1,), (1,)), ((), ())), preferred_element_type=F32)
        s = jnp.where(kvalid, s + bias_ref[g * HEADS_PER_GROUP + 2 * pair + h], NEG)
        mx = jnp.max(s, axis=-1, keepdims=True)
        p = jnp.exp(s - mx)
        sums.append(jnp.sum(p, axis=-1, keepdims=True))
        maxs.append(mx)
        outs.append(jnp.dot(p.astype(BF16), v, preferred_element_type=F32))
      o = jnp.where(lane_lo, outs[0], outs[1])
      mx = jnp.where(lane_lo, maxs[0], maxs[1])
      sm = jnp.where(lane_lo, sums[0], sums[1])
      if g == group_order[0]:
        acc_ref[rows_q, :] = o
        m_ref[rows_q, :] = mx
        l_ref[rows_q, :] = sm
      else:
        m_old = m_ref[rows_q, :]
        m_new = jnp.maximum(m_old, mx)
        a_old = jnp.exp(m_old - m_new)
        a_new = jnp.exp(mx - m_new)
        acc_ref[rows_q, :] = acc_ref[rows_q, :] * a_old + o * a_new
        l_ref[rows_q, :] = l_ref[rows_q, :] * a_old + sm * a_new
        m_ref[rows_q, :] = m_new
      return carry

    lax.fori_loop(0, SEQ // ATT_BQ, block, 0, unroll=ATT_UNROLL)

  o_ref[...] = (acc_ref[...] / l_ref[...]).astype(BF16)


def _dil_attn(groups, rel_bias, bkt, batch):
  per_group = ATT_OUT // LANES

  def col(g, kind):
    d = DILATIONS[g]
    if d == 1:
      return pl.BlockSpec((SEQ, LANES), lambda b, pr, tab: (b, kind * per_group + pr))
    return pl.BlockSpec((1, d, SEQ // d, LANES),
                        lambda b, pr, tab: (b, 0, 0, kind * per_group + pr))

  in_specs = [col(g, kind) for g in range(N_GROUPS) for kind in range(3)]
  in_specs += [pl.BlockSpec(bkt.shape, lambda b, pr, tab: (0, 0, 0))]
  key_rows = SEQ + 2 * N_SIDE * max(DILATIONS)
  grid_spec = pltpu.PrefetchScalarGridSpec(
      num_scalar_prefetch=1,
      grid=(batch, per_group),
      in_specs=in_specs,
      out_specs=pl.BlockSpec((SEQ, LANES), lambda b, pr, tab: (b, pr)),
      scratch_shapes=[pltpu.VMEM((key_rows, LANES), BF16), pltpu.VMEM((key_rows, LANES), BF16),
                      pltpu.VMEM((SEQ, LANES), F32), pltpu.VMEM((SEQ, LANES), F32),
                      pltpu.VMEM((SEQ, LANES), F32),
                      pltpu.VMEM((N_GROUPS * HEADS_PER_GROUP, ATT_BQ, ATT_BK), F32)])
  operands = [groups[g] for g in range(N_GROUPS) for _ in range(3)]
  return pl.pallas_call(
      _dil_attn_kernel,
      grid_spec=grid_spec,
      out_shape=jax.ShapeDtypeStruct((batch * SEQ, ATT_OUT), BF16),
      compiler_params=pltpu.CompilerParams(
          dimension_semantics=("arbitrary", "arbitrary"), vmem_limit_bytes=VMEM_LIMIT_BYTES),
      name="dil_attn",
  )(rel_bias, *operands, bkt)


def _filter_features():
  t = jnp.linspace(0.0, 1.0, SEQ, dtype=F32)[:, None]
  w = (2.0 * math.pi / SEQ) * jnp.arange(SEQ, dtype=F32)[:, None]
  f = jnp.linspace(1e-4, HY_POS_BANDS - 1, HY_POS_BANDS, dtype=F32)[None]
  z = jnp.concatenate([t, jnp.cos(f * w), -jnp.sin(f * w)], axis=-1)
  return jnp.pad(z, ((0, 0), (0, LANES - HY_EMB)))


def _decay_rates():
  max_decay = math.log(HY_DECAY_TARGET) / HY_FAST_DECAY_PCT
  min_decay = math.log(HY_DECAY_TARGET) / HY_SLOW_DECAY_PCT
  return jnp.abs(jnp.linspace(min_decay, max_decay, HY_WIDTH, dtype=F32))[None]


def kernel(x, rel_bias, ffn1_norm, ffn1_w_gate, ffn1_w_up, ffn1_w_down, mix_norm, w_in, w_gate, b_gate, hy_conv_w, hy_conv_b, hy_filt_w1, hy_filt_b1, hy_filt_w2, hy_filt_b2, hy_filt_w3, hy_skip, q_norm, k_norm, w_hy_proj, w_at_proj, w_out, ffn2_norm, ffn2_w_gate, ffn2_w_up, ffn2_w_down):
  batch, seq, d_model = x.shape
  assert (seq, d_model) == (SEQ, D_MODEL) and batch % 2 == 0
  depth = ffn1_norm.shape[0]
  mk, mki, md, mdi = _dft_constants()
  zfeat = _filter_features()
  deltas = _decay_rates()
  bkt = _bucket_maps()
  bf = lambda w: w.astype(BF16)
  row = lambda v: v.reshape(1, -1)

  xt = x.reshape(batch * seq, d_model)
  for l in range(depth):
    xt, p, *groups = _ffn_proj(
        xt, row(ffn1_norm[l]), bf(ffn1_w_gate[l]), bf(ffn1_w_up[l]), bf(ffn1_w_down[l]),
        row(mix_norm[l]), bf(w_in[l]), row(jnp.tile(q_norm[l], 2)),
        row(jnp.tile(k_norm[l], 2)), batch)
    w1 = jnp.pad(hy_filt_w1[l], ((0, LANES - HY_EMB), (0, 0)))
    hspec = _hy_filter(zfeat, w1, row(hy_filt_b1[l]), hy_filt_w2[l], row(hy_filt_b2[l]),
                       hy_filt_w3[l], deltas, mk, md)
    cw = jnp.concatenate([hy_conv_w[l], hy_conv_b[l][None]], axis=0)
    cw = cw.reshape(4, 3, HY_WIDTH).transpose(1, 0, 2)
    yhy = _hyena(p, cw, hy_skip[l], hspec, mk, mki, md, mdi, batch)
    yat = _dil_attn(groups, rel_bias, bkt, batch)
    xt = _merge_ffn(xt, yhy, yat, row(mix_norm[l]), bf(w_gate[l]), row(b_gate[l]),
                    bf(w_hy_proj[l]), bf(w_at_proj[l]), bf(w_out[l]), row(ffn2_norm[l]),
                    bf(ffn2_w_gate[l]), bf(ffn2_w_up[l]), bf(ffn2_w_down[l]))
  return xt.reshape(batch, seq, d_model)
```
